```python
import math
import jax, jax.numpy as jnp
from jax import lax
import numpy as np

D_MODEL = 4096
BATCH = 4
SEQ = 2048
DEPTH = 2
DEC_BATCH = 1
DEC_SEQ = 16384
PAST_LEN = 128

HEAD_DIM = 128
GRID_W = 64
Q_BLOCK = 128
DIL_GROUPS = ((128, 1), (512, 4), (2048, 16))
A_HEADS_PER_GROUP = 4
A_HEADS = 12
B_HEADS = 8
B_QK_DIM = 64
C_HEADS = 12
C_KV_HEADS = 4
ROPE_THETA = 10000.0
MIX_HEADS = 32
MIX_WIDTH = 4096
A_W = 1536
B_QK_W = 1024
B_V_W = 1024
C_Q_W = 1536
C_KV_W = 512
IN_WIDTH = 10240
N_EXPERTS = 64
TOP_K = 8
N_GROUPS = 8
TOPK_GROUPS = 4
EXPERT_FF = 1152
ROUTED_SCALE = 2.5
MOE_BLOCK = 256
DEEPNORM_ALPHA = (2 * DEPTH) ** 0.25
DEEPNORM_BETA = (8 * DEPTH) ** -0.25
LN_EPS = 1e-5
RMS_EPS = 1e-6
NEG_BIG = -1e30

kernel_name = 'hybrid_dilated_diff_axial_moe_encoder'


def layer_norm(x, g, b):
    xf = x.astype(jnp.float32)
    mu = jnp.mean(xf, -1, keepdims=True)
    var = jnp.mean(jnp.square(xf - mu), -1, keepdims=True)
    y = (xf - mu) * lax.rsqrt(var + LN_EPS) * g.astype(jnp.float32) + b.astype(jnp.float32)
    return y.astype(x.dtype)


def rms_norm(x, g):
    xf = x.astype(jnp.float32)
    y = xf * lax.rsqrt(jnp.mean(jnp.square(xf), -1, keepdims=True) + RMS_EPS) * g.astype(jnp.float32)
    return y.astype(x.dtype)


def alibi_slopes(n):
    return 2.0 ** (-8.0 * jnp.arange(1, n + 1, dtype=jnp.float32) / n)


def dilated_group_attention(q, k, v, dilation, half, slopes):
    Bsz, T, H, hd = q.shape
    L = T // dilation
    N = Bsz * dilation

    def split(a):
        return a.reshape(Bsz, L, dilation, H, hd).transpose(0, 2, 1, 3, 4).reshape(N, L, H, hd)

    qs, ks, vs = split(q), split(k), split(v)
    blk = half
    nb = -(-L // blk)
    Lp = nb * blk
    qs = jnp.pad(qs, ((0, 0), (0, Lp - L), (0, 0), (0, 0)))
    pad_k = ((0, 0), (blk, Lp - L + blk), (0, 0), (0, 0))
    ks = jnp.pad(ks, pad_k)
    vs = jnp.pad(vs, pad_k)
    key_idx = jnp.arange(nb)[:, None] * blk + jnp.arange(3 * blk)[None, :]
    kb = ks[:, key_idx]
    vb = vs[:, key_idx]
    qb = qs.reshape(N, nb, blk, H, hd)
    s = jnp.einsum('nbqhd,nbkhd->nbhqk', qb, kb, preferred_element_type=jnp.float32) * (hd ** -0.5)
    qpos = jnp.arange(Lp).reshape(nb, blk)
    kpos = key_idx - blk
    dist = jnp.abs(qpos[:, :, None] - kpos[:, None, :])
    valid = (dist <= half) & (kpos[:, None, :] >= 0) & (kpos[:, None, :] < L)
    bias = -slopes[None, :, None, None] * (dilation * dist).astype(jnp.float32)[:, None]
    s = jnp.where(valid[:, None], s + bias, NEG_BIG)
    m = jnp.max(s, -1, keepdims=True)
    p = jnp.exp(s - m)
    den = jnp.sum(p, -1, keepdims=True)
    o = jnp.einsum('nbhqk,nbkhd->nbqhd', (p / den).astype(vb.dtype), vb, preferred_element_type=jnp.float32)
    lse = (m + jnp.log(den))[..., 0]
    o = o.reshape(N, Lp, H, hd)[:, :L]
    lse = lse.transpose(0, 1, 3, 2).reshape(N, Lp, H)[:, :L]
    o = o.reshape(Bsz, dilation, L, H, hd).transpose(0, 2, 1, 3, 4).reshape(Bsz, T, H, hd)
    lse = lse.reshape(Bsz, dilation, L, H).transpose(0, 2, 1, 3).reshape(Bsz, T, H)
    return o, lse


def diff_attention(q1, q2, k1, k2, v, lam, slopes):
    Bsz, T, H, dk = q1.shape
    scale = dk ** -0.5
    kpos = jnp.arange(T, dtype=jnp.float32)

    def block(i):
        start = i * Q_BLOCK
        qb1 = lax.dynamic_slice_in_dim(q1, start, Q_BLOCK, axis=1)
        qb2 = lax.dynamic_slice_in_dim(q2, start, Q_BLOCK, axis=1)
        qpos = start.astype(jnp.float32) + jnp.arange(Q_BLOCK, dtype=jnp.float32)
        bias = -slopes[:, None, None] * jnp.abs(qpos[:, None] - kpos[None, :])
        a1 = jax.nn.softmax(jnp.einsum('bqhd,bkhd->bhqk', qb1, k1, preferred_element_type=jnp.float32) * scale + bias, axis=-1)
        a2 = jax.nn.softmax(jnp.einsum('bqhd,bkhd->bhqk', qb2, k2, preferred_element_type=jnp.float32) * scale + bias, axis=-1)
        w = (a1 - lam * a2).astype(v.dtype)
        return jnp.einsum('bhqk,bkhd->bqhd', w, v, preferred_element_type=jnp.float32)

    outs = lax.map(block, jnp.arange(T // Q_BLOCK))
    return jnp.moveaxis(outs, 0, 1).reshape(Bsz, T, H, v.shape[-1])


def axial_rope_tables(T):
    rows = T // GRID_W
    r, c = jnp.meshgrid(jnp.arange(rows, dtype=jnp.float32), jnp.arange(GRID_W, dtype=jnp.float32), indexing='ij')
    r, c = r.reshape(-1), c.reshape(-1)
    half = HEAD_DIM // 2
    inv_freq = ROPE_THETA ** (-jnp.arange(0, half, 2, dtype=jnp.float32) / half)
    ang = jnp.concatenate([r[:, None] * inv_freq, c[:, None] * inv_freq], -1)
    return jnp.cos(ang), jnp.sin(ang)


def apply_rope(x, cos, sin):
    xf = x.astype(jnp.float32).reshape(x.shape[:-1] + (HEAD_DIM // 2, 2))
    x1, x2 = xf[..., 0], xf[..., 1]
    c, s = cos[None, :, None, :], sin[None, :, None, :]
    out = jnp.stack([x1 * c - x2 * s, x1 * s + x2 * c], -1).reshape(x.shape)
    return out.astype(x.dtype)


def gqa_attention(q, k, v):
    Bsz, T, Hq, hd = q.shape
    Hkv = k.shape[2]
    G = Hq // Hkv
    qg = q.reshape(Bsz, T, Hkv, G, hd)
    scale = hd ** -0.5

    def block(i):
        qb = lax.dynamic_slice_in_dim(qg, i * Q_BLOCK, Q_BLOCK, axis=1)
        s = jnp.einsum('bqhgd,bkhd->bhgqk', qb, k, preferred_element_type=jnp.float32) * scale
        p = jax.nn.softmax(s, axis=-1).astype(v.dtype)
        return jnp.einsum('bhgqk,bkhd->bqhgd', p, v, preferred_element_type=jnp.float32)

    outs = lax.map(block, jnp.arange(T // Q_BLOCK))
    return jnp.moveaxis(outs, 0, 1).reshape(Bsz, T, Hq, hd)


def token_mixer(x, lam_init, w_in_l, w_out_l, out_gain_l, q_norm_l, k_norm_l, lq1, lk1, lq2, lk2):
    Bsz, T, _ = x.shape
    proj = jnp.einsum('btd,de->bte', x, w_in_l)
    sizes = (A_W, A_W, A_W, B_QK_W, B_QK_W, B_V_W, C_Q_W, C_KV_W, C_KV_W)
    points = [sum(sizes[:i + 1]) for i in range(len(sizes) - 1)]
    qa, ka, va, qb, kb, vb, qc, kc, vc = jnp.split(proj, points, axis=-1)

    qa = qa.reshape(Bsz, T, A_HEADS, HEAD_DIM)
    ka = ka.reshape(Bsz, T, A_HEADS, HEAD_DIM)
    va = va.reshape(Bsz, T, A_HEADS, HEAD_DIM)
    slopes_a = alibi_slopes(A_HEADS).reshape(len(DIL_GROUPS), A_HEADS_PER_GROUP)
    outs, lses = [], []
    for g, (window, dil) in enumerate(DIL_GROUPS):
        sl = slice(g * A_HEADS_PER_GROUP, (g + 1) * A_HEADS_PER_GROUP)
        o, s = dilated_group_attention(qa[:, :, sl], ka[:, :, sl], va[:, :, sl], dil, (window // 2) // dil, slopes_a[g])
        outs.append(o)
        lses.append(s)
    alpha_g = jax.nn.softmax(jnp.stack(lses, 0), axis=0)
    out_a = jnp.stack(outs, 0) * alpha_g[..., None]
    out_a = jnp.moveaxis(out_a, 0, 2).reshape(Bsz, T, A_HEADS, HEAD_DIM)

    qb = qb.reshape(Bsz, T, B_HEADS, 2, B_QK_DIM)
    kb = kb.reshape(Bsz, T, B_HEADS, 2, B_QK_DIM)
    vb = vb.reshape(Bsz, T, B_HEADS, HEAD_DIM)
    lam = (jnp.exp(jnp.sum(lq1.astype(jnp.float32) * lk1.astype(jnp.float32)))
           - jnp.exp(jnp.sum(lq2.astype(jnp.float32) * lk2.astype(jnp.float32))) + lam_init)
    out_b = diff_attention(qb[..., 0, :], qb[..., 1, :], kb[..., 0, :], kb[..., 1, :], vb, lam, alibi_slopes(B_HEADS))

    cos, sin = axial_rope_tables(T)
    qc = apply_rope(rms_norm(qc.reshape(Bsz, T, C_HEADS, HEAD_DIM), q_norm_l), cos, sin)
    kc = apply_rope(rms_norm(kc.reshape(Bsz, T, C_KV_HEADS, HEAD_DIM), k_norm_l), cos, sin)
    vc = vc.reshape(Bsz, T, C_KV_HEADS, HEAD_DIM)
    out_c = gqa_attention(qc, kc, vc)

    heads = jnp.concatenate([out_a, out_b, out_c], axis=2)
    heads = heads * lax.rsqrt(jnp.mean(jnp.square(heads), -1, keepdims=True) + RMS_EPS)
    head_scale = jnp.concatenate([jnp.ones((A_HEADS,), jnp.float32),
                                  jnp.full((B_HEADS,), 1.0 - lam_init, jnp.float32),
                                  jnp.ones((C_HEADS,), jnp.float32)])
    merged = (heads * head_scale[:, None]).reshape(Bsz, T, MIX_WIDTH) * out_gain_l.astype(jnp.float32)
    return jnp.einsum('bte,ed->btd', merged.astype(x.dtype), w_out_l)


def moe_ffn(x, router_w, router_bias, w_gate, w_up, w_down, sh_gate, sh_up, sh_down):
    Bsz, T, D = x.shape
    xt = x.reshape(-1, D)
    N = xt.shape[0]
    scores = jax.nn.sigmoid(jnp.einsum('nd,de->ne', xt, router_w, preferred_element_type=jnp.float32))
    biased = scores + router_bias.astype(jnp.float32)
    grp = biased.reshape(N, N_GROUPS, N_EXPERTS // N_GROUPS)
    grp_score = jnp.sum(lax.top_k(grp, 2)[0], -1)
    _, top_groups = lax.top_k(grp_score, TOPK_GROUPS)
    gmask = jnp.sum(jax.nn.one_hot(top_groups, N_GROUPS, dtype=jnp.float32), 1) > 0
    emask = jnp.repeat(gmask, N_EXPERTS // N_GROUPS, axis=1)
    _, idx = lax.top_k(jnp.where(emask, biased, -jnp.inf), TOP_K)
    w_sel = jnp.take_along_axis(scores, idx, axis=1)
    w_sel = w_sel / jnp.sum(w_sel, -1, keepdims=True) * ROUTED_SCALE

    NK = N * TOP_K
    flat_e = idx.reshape(-1)
    flat_t = jnp.repeat(jnp.arange(N, dtype=jnp.int32), TOP_K)
    flat_w = w_sel.reshape(-1)
    order = jnp.argsort(flat_e)
    e_s, t_s, w_s = flat_e[order], flat_t[order], flat_w[order]
    counts = jnp.bincount(flat_e, length=N_EXPERTS)
    padded = (counts + MOE_BLOCK - 1) // MOE_BLOCK * MOE_BLOCK
    starts = jnp.cumsum(counts) - counts
    pends = jnp.cumsum(padded)
    pstarts = pends - padded
    dest = pstarts[e_s] + jnp.arange(NK) - starts[e_s]
    n_blocks = -(-NK // MOE_BLOCK) + N_EXPERTS
    buf_t = jnp.full((n_blocks * MOE_BLOCK,), N, jnp.int32).at[dest].set(t_s)
    buf_w = jnp.zeros((n_blocks * MOE_BLOCK,), jnp.float32).at[dest].set(w_s)
    block_e = jnp.minimum(jnp.searchsorted(pends, jnp.arange(n_blocks) * MOE_BLOCK, side='right'), N_EXPERTS - 1)
    x_pad = jnp.concatenate([xt, jnp.zeros((1, D), xt.dtype)], 0)

    def step(acc, inp):
        rows, wts, e = inp
        xb = x_pad[rows]
        h = jax.nn.silu(jnp.dot(xb, w_gate[e], preferred_element_type=jnp.float32)) * jnp.dot(xb, w_up[e], preferred_element_type=jnp.float32)
        yb = jnp.dot(h.astype(x.dtype), w_down[e], preferred_element_type=jnp.float32) * wts[:, None]
        return acc.at[rows].add(yb), None

    acc, _ = lax.scan(step, jnp.zeros((N + 1, D), jnp.float32),
                      (buf_t.reshape(n_blocks, MOE_BLOCK), buf_w.reshape(n_blocks, MOE_BLOCK), block_e))
    hs = jax.nn.silu(jnp.dot(xt, sh_gate, preferred_element_type=jnp.float32)) * jnp.dot(xt, sh_up, preferred_element_type=jnp.float32)
    shared = jnp.dot(hs.astype(x.dtype), sh_down, preferred_element_type=jnp.float32)
    return (acc[:N] + shared).astype(x.dtype).reshape(Bsz, T, D)


def encoder_trunk(x, ln_in_g, ln_in_b, w_in, w_out, out_gain, q_norm, k_norm, lambda_q1, lambda_k1,
                  lambda_q2, lambda_k2, ln1_g, ln1_b, router_w, router_bias, w_gate, w_up, w_down,
                  shared_gate, shared_up, shared_down, ln2_g, ln2_b):
    x = layer_norm(x, ln_in_g, ln_in_b)
    for l in range(DEPTH):
        lam_init = 0.8 - 0.6 * math.exp(-0.3 * l)
        mix = token_mixer(x, lam_init, w_in[l], w_out[l], out_gain[l], q_norm[l], k_norm[l],
                          lambda_q1[l], lambda_k1[l], lambda_q2[l], lambda_k2[l])
        x = layer_norm(DEEPNORM_ALPHA * x + mix, ln1_g[l], ln1_b[l])
        ffn = moe_ffn(x, router_w[l], router_bias[l], w_gate[l], w_up[l], w_down[l],
                      shared_gate[l], shared_up[l], shared_down[l])
        x = layer_norm(DEEPNORM_ALPHA * x + ffn, ln2_g[l], ln2_b[l])
    return x


def setup_inputs(seed: int = 0) -> dict:
    key = jax.random.key(seed)
    ks = jax.random.split(key, 26)

    def nrm(k, shape, scale):
        return jax.random.normal(k, shape, jnp.float32) * scale

    return {
        'x_prompt': nrm(ks[0], (BATCH, SEQ, D_MODEL), 1.0),
        'x_sample': nrm(ks[1], (DEC_BATCH, DEC_SEQ, D_MODEL), 1.0),
        'ln_in_g': 1.0 + nrm(ks[2], (D_MODEL,), 0.02),
        'ln_in_b': nrm(ks[3], (D_MODEL,), 0.02),
        'w_in': nrm(ks[4], (DEPTH, D_MODEL, IN_WIDTH), D_MODEL ** -0.5),
        'w_out': nrm(ks[5], (DEPTH, MIX_WIDTH, D_MODEL), MIX_WIDTH ** -0.5 * DEEPNORM_BETA),
        'out_gain': 1.0 + nrm(ks[6], (DEPTH, MIX_WIDTH), 0.02),
        'q_norm': 1.0 + nrm(ks[7], (DEPTH, HEAD_DIM), 0.02),
        'k_norm': 1.0 + nrm(ks[8], (DEPTH, HEAD_DIM), 0.02),
        'lambda_q1': nrm(ks[9], (DEPTH, B_QK_DIM), 0.1),
        'lambda_k1': nrm(ks[10], (DEPTH, B_QK_DIM), 0.1),
        'lambda_q2': nrm(ks[11], (DEPTH, B_QK_DIM), 0.1),
        'lambda_k2': nrm(ks[12], (DEPTH, B_QK_DIM), 0.1),
        'ln1_g': 1.0 + nrm(ks[13], (DEPTH, D_MODEL), 0.02),
        'ln1_b': nrm(ks[14], (DEPTH, D_MODEL), 0.02),
        'router_w': nrm(ks[15], (DEPTH, D_MODEL, N_EXPERTS), D_MODEL ** -0.5),
        'router_bias': nrm(ks[16], (DEPTH, N_EXPERTS), 0.01),
        'w_gate': nrm(ks[17], (DEPTH, N_EXPERTS, D_MODEL, EXPERT_FF), D_MODEL ** -0.5),
        'w_up': nrm(ks[18], (DEPTH, N_EXPERTS, D_MODEL, EXPERT_FF), D_MODEL ** -0.5),
        'w_down': nrm(ks[19], (DEPTH, N_EXPERTS, EXPERT_FF, D_MODEL), EXPERT_FF ** -0.5 * DEEPNORM_BETA),
        'shared_gate': nrm(ks[20], (DEPTH, D_MODEL, EXPERT_FF), D_MODEL ** -0.5),
        'shared_up': nrm(ks[21], (DEPTH, D_MODEL, EXPERT_FF), D_MODEL ** -0.5),
        'shared_down': nrm(ks[22], (DEPTH, EXPERT_FF, D_MODEL), EXPERT_FF ** -0.5 * DEEPNORM_BETA),
        'ln2_g': 1.0 + nrm(ks[23], (DEPTH, D_MODEL), 0.02),
        'ln2_b': nrm(ks[24], (DEPTH, D_MODEL), 0.02),
    }


def reference(x_prompt, x_sample, ln_in_g, ln_in_b, w_in, w_out, out_gain, q_norm, k_norm,
              lambda_q1, lambda_k1, lambda_q2, lambda_k2, ln1_g, ln1_b, router_w, router_bias,
              w_gate, w_up, w_down, shared_gate, shared_up, shared_down, ln2_g, ln2_b):
    y_prompt = encoder_trunk(x_prompt, ln_in_g, ln_in_b, w_in, w_out, out_gain, q_norm, k_norm,
                             lambda_q1, lambda_k1, lambda_q2, lambda_k2, ln1_g, ln1_b, router_w,
                             router_bias, w_gate, w_up, w_down, shared_gate, shared_up, shared_down,
                             ln2_g, ln2_b)
    y_sample = encoder_trunk(x_sample, ln_in_g, ln_in_b, w_in, w_out, out_gain, q_norm, k_norm,
                             lambda_q1, lambda_k1, lambda_q2, lambda_k2, ln1_g, ln1_b, router_w,
                             router_bias, w_gate, w_up, w_down, shared_gate, shared_up, shared_down,
                             ln2_g, ln2_b)
    return (y_prompt, y_sample)
```

```python
import functools
import math

import jax
import jax.numpy as jnp
from jax import lax
from jax.experimental import pallas as pl
from jax.experimental.pallas import tpu as pltpu

F32 = jnp.float32
BF16 = jnp.bfloat16

HEAD_DIM = 128
GRID_W = 64
DIL_GROUPS = ((128, 1), (512, 4), (2048, 16))
A_HEADS_PER_GROUP = 4
A_HEADS = 12
B_HEADS = 8
B_QK_DIM = 64
C_HEADS = 12
C_KV_HEADS = 4
C_GROUP = C_HEADS // C_KV_HEADS
ROPE_THETA = 10000.0
A_W = A_HEADS * HEAD_DIM
B_QK_W = B_HEADS * 2 * B_QK_DIM
B_V_W = B_HEADS * HEAD_DIM
C_Q_W = C_HEADS * HEAD_DIM
C_KV_W = C_KV_HEADS * HEAD_DIM
IN_WIDTH = 3 * A_W + 2 * B_QK_W + B_V_W + C_Q_W + 2 * C_KV_W
MIX_WIDTH = (A_HEADS + B_HEADS + C_HEADS) * HEAD_DIM
OFF_QA, OFF_KA, OFF_VA = 0, A_W, 2 * A_W
OFF_QB = 3 * A_W
OFF_KB = OFF_QB + B_QK_W
OFF_VB = OFF_KB + B_QK_W
OFF_QC = OFF_VB + B_V_W
OFF_KC = OFF_QC + C_Q_W
OFF_VC = OFF_KC + C_KV_W
TOP_K = 8
N_GROUPS = 8
TOPK_GROUPS = 4
ROUTED_SCALE = 2.5
LN_EPS = 1e-5
RMS_EPS = 1e-6
NEG_BIG = -1e30

V7X_VMEM_BYTES = 64 * 1024 * 1024
VMEM_LIMIT = 52 * 1024 * 1024


def _cparams(sem):
    return pltpu.CompilerParams(dimension_semantics=sem, vmem_limit_bytes=VMEM_LIMIT)


def _tile(n, pref):
    t = min(n, pref)
    while n % t:
        t //= 2
    return t


def _ln_kernel(alpha, n_add, *refs):
    x_ref = refs[0]
    add_refs = refs[1:1 + n_add]
    g_ref, b_ref, of_ref, ob_ref = refs[1 + n_add:]
    x = x_ref[...]
    if n_add:
        x = alpha * x
        for a in add_refs:
            x = x + a[...].astype(F32)
    mu = jnp.mean(x, axis=-1, keepdims=True)
    xc = x - mu
    var = jnp.mean(xc * xc, axis=-1, keepdims=True)
    y = xc * lax.rsqrt(var + LN_EPS) * g_ref[...] + b_ref[...]
    of_ref[...] = y
    ob_ref[...] = y.astype(BF16)


def _layer_norm(x, adds, alpha, g, b):
    n, d = x.shape
    tr = _tile(n, 256)
    row = pl.BlockSpec((tr, d), lambda i: (i, 0))
    vec = pl.BlockSpec((1, d), lambda i: (0, 0))
    return pl.pallas_call(
        functools.partial(_ln_kernel, alpha, len(adds)),
        out_shape=(jax.ShapeDtypeStruct((n, d), F32), jax.ShapeDtypeStruct((n, d), BF16)),
        grid=(n // tr,),
        in_specs=[row] * (1 + len(adds)) + [vec, vec],
        out_specs=(row, row),
        compiler_params=_cparams(("arbitrary",)),
        name="layer_norm",
    )(x, *adds, g.reshape(1, d), b.reshape(1, d))


def _mm_kernel(ksizes, *refs):
    a_refs = refs[:len(ksizes)]
    w_ref, o_ref = refs[len(ksizes):]
    acc = None
    off = 0
    for a_ref, kk in zip(a_refs, ksizes):
        part = jnp.dot(a_ref[...], w_ref[off:off + kk, :], preferred_element_type=F32)
        acc = part if acc is None else acc + part
        off += kk
    o_ref[...] = acc.astype(o_ref.dtype)


def _matmul(parts, w, out_dtype):
    m = parts[0].shape[0]
    k, n = w.shape
    ksizes = tuple(p.shape[1] for p in parts)
    assert sum(ksizes) == k
    tm = _tile(m, 512)
    tn = _tile(n, 1024)
    in_specs = [pl.BlockSpec((tm, kk), lambda j, i: (i, 0)) for kk in ksizes]
    in_specs.append(pl.BlockSpec((k, tn), lambda j, i: (0, j)))
    return pl.pallas_call(
        functools.partial(_mm_kernel, ksizes),
        out_shape=jax.ShapeDtypeStruct((m, n), out_dtype),
        grid=(n // tn, m // tm),
        in_specs=in_specs,
        out_specs=pl.BlockSpec((tm, tn), lambda j, i: (i, j)),
        compiler_params=_cparams(("arbitrary", "arbitrary")),
        name="dense_projection",
    )(*parts, w)


def _head_norm(o, gain, head_scale):
    ms = jnp.mean(o * o, axis=-1, keepdims=True)
    return o * lax.rsqrt(ms + RMS_EPS) * (gain * head_scale)


def _window_attn_kernel(slopes, dil, seq_l, blk, half, q_ref, kp_ref, kc_ref, kn_ref,
                        vp_ref, vc_ref, vn_ref, o_ref, lse_ref):
    lb = pl.program_id(2)
    ii = lax.broadcasted_iota(jnp.int32, (blk, 3 * blk), 0)
    jj = lax.broadcasted_iota(jnp.int32, (blk, 3 * blk), 1)
    dist = jnp.abs(blk + ii - jj)
    kpos = (lb - 1) * blk + jj
    valid = (dist <= half) & (kpos >= 0) & (kpos < seq_l)
    tok_dist = (dist * dil).astype(F32)
    lane = lax.broadcasted_iota(jnp.int32, (blk, HEAD_DIM), 1)
    lse_tile = jnp.zeros((blk, HEAD_DIM), F32)
    scale = HEAD_DIM ** -0.5
    for h in range(A_HEADS_PER_GROUP):
        hs = slice(h * HEAD_DIM, (h + 1) * HEAD_DIM)
        q = q_ref[:, hs]
        k = jnp.concatenate([kp_ref[:, hs], kc_ref[:, hs], kn_ref[:, hs]], axis=0)
        v = jnp.concatenate([vp_ref[:, hs], vc_ref[:, hs], vn_ref[:, hs]], axis=0)
        s = lax.dot_general(q, k, (((1,), (1,)), ((), ())), preferred_element_type=F32) * scale
        s = jnp.where(valid, s - slopes[h] * tok_dist, NEG_BIG)
        m = jnp.max(s, axis=-1, keepdims=True)
        p = jnp.exp(s - m)
        den = jnp.sum(p, axis=-1, keepdims=True)
        o = jnp.dot(p.astype(BF16), v, preferred_element_type=F32)
        o_ref[:, hs] = o / den
        lse_tile = jnp.where(lane == h, m + jnp.log(den), lse_tile)
    lse_ref[...] = lse_tile


def _window_attention(proj, row0, nseq, seq_t, group):
    window, dil = DIL_GROUPS[group]
    half = (window // 2) // dil
    n_all = proj.shape[0]
    seq_l = seq_t // dil
    blk = _tile(seq_l, 128)
    nlb = seq_l // blk
    base = row0 // dil // blk
    cw = A_HEADS_PER_GROUP * HEAD_DIM
    cpr = IN_WIDTH // cw
    proj_v = proj.reshape(n_all // dil, dil * IN_WIDTH)
    slopes = tuple(2.0 ** (-8.0 * (group * A_HEADS_PER_GROUP + h + 1) / A_HEADS)
                   for h in range(A_HEADS_PER_GROUP))

    def spec(col0, shift):
        def imap(b, r, lb):
            nb = jnp.clip(lb + shift, 0, nlb - 1)
            return (base + b * nlb + nb, r * cpr + col0 // cw + group)
        return pl.BlockSpec((blk, cw), imap)

    rows = nseq * seq_l
    o, lse = pl.pallas_call(
        functools.partial(_window_attn_kernel, slopes, dil, seq_l, blk, half),
        out_shape=(jax.ShapeDtypeStruct((rows, dil * cw), F32),
                   jax.ShapeDtypeStruct((rows, dil * HEAD_DIM), F32)),
        grid=(nseq, dil, nlb),
        in_specs=[spec(OFF_QA, 0),
                  spec(OFF_KA, -1), spec(OFF_KA, 0), spec(OFF_KA, 1),
                  spec(OFF_VA, -1), spec(OFF_VA, 0), spec(OFF_VA, 1)],
        out_specs=(pl.BlockSpec((blk, cw), lambda b, r, lb: (b * nlb + lb, r)),
                   pl.BlockSpec((blk, HEAD_DIM), lambda b, r, lb: (b * nlb + lb, r))),
        compiler_params=_cparams(("arbitrary", "arbitrary", "arbitrary")),
        name="window_attention",
    )(proj_v, proj_v, proj_v, proj_v, proj_v, proj_v, proj_v)
    return o.reshape(nseq * seq_t, cw), lse.reshape(nseq * seq_t, HEAD_DIM)


def _window_merge_kernel(o0_ref, o1_ref, o2_ref, l0_ref, l1_ref, l2_ref, gain_ref, out_ref):
    o_refs = (o0_ref, o1_ref, o2_ref)
    lses = (l0_ref[...], l1_ref[...], l2_ref[...])
    mx = jnp.maximum(jnp.maximum(lses[0], lses[1]), lses[2])
    es = [jnp.exp(l - mx) for l in lses]
    tot = es[0] + es[1] + es[2]
    for g in range(len(DIL_GROUPS)):
        alpha = es[g] / tot
        for h in range(A_HEADS_PER_GROUP):
            hs = slice(h * HEAD_DIM, (h + 1) * HEAD_DIM)
            c0 = (g * A_HEADS_PER_GROUP + h) * HEAD_DIM
            o = o_refs[g][:, hs] * alpha[:, h:h + 1]
            out_ref[:, c0:c0 + HEAD_DIM] = _head_norm(o, gain_ref[:, c0:c0 + HEAD_DIM], 1.0).astype(BF16)


def _window_merge(outs, lses, gain_a):
    n = outs[0].shape[0]
    cw = A_HEADS_PER_GROUP * HEAD_DIM
    tr = _tile(n, 256)
    return pl.pallas_call(
        _window_merge_kernel,
        out_shape=jax.ShapeDtypeStruct((n, A_W), BF16),
        grid=(n // tr,),
        in_specs=[pl.BlockSpec((tr, cw), lambda i: (i, 0))] * 3
        + [pl.BlockSpec((tr, HEAD_DIM), lambda i: (i, 0))] * 3
        + [pl.BlockSpec((1, A_W), lambda i: (0, 0))],
        out_specs=pl.BlockSpec((tr, A_W), lambda i: (i, 0)),
        compiler_params=_cparams(("arbitrary",)),
        name="window_merge",
    )(*outs, *lses, gain_a)


def _online_step(s, v, m_ref, l_ref, acc_ref):
    m_old = m_ref[...]
    m_new = jnp.maximum(m_old, jnp.max(s, axis=-1, keepdims=True))
    a = jnp.exp(m_old - m_new)
    p = jnp.exp(s - m_new)
    l_ref[...] = a * l_ref[...] + jnp.sum(p, axis=-1, keepdims=True)
    acc_ref[...] = a * acc_ref[...] + jnp.dot(p.astype(BF16), v, preferred_element_type=F32)
    m_ref[...] = m_new


def _init_softmax(m_ref, l_ref, acc_ref):
    m_ref[...] = jnp.full(m_ref.shape, -jnp.inf, F32)
    l_ref[...] = jnp.zeros(l_ref.shape, F32)
    acc_ref[...] = jnp.zeros(acc_ref.shape, F32)


def _diff_attn_kernel(tq, tk, head_scale, slope_ref, lam_ref, gain_ref, q_ref, k_ref, v_ref, o_ref,
                      m1, l1, a1, m2, l2, a2):
    h = pl.program_id(1)
    qi = pl.program_id(2)
    ki = pl.program_id(3)

    @pl.when(ki == 0)
    def _():
        _init_softmax(m1, l1, a1)
        _init_softmax(m2, l2, a2)

    ii = lax.broadcasted_iota(jnp.int32, (tq, tk), 0)
    jj = lax.broadcasted_iota(jnp.int32, (tq, tk), 1)
    bias = jnp.abs(qi * tq - ki * tk + ii - jj).astype(F32) * (-slope_ref[h])
    q = q_ref[...]
    k = k_ref[...]
    v = v_ref[...]
    scale = B_QK_DIM ** -0.5
    dn = (((1,), (1,)), ((), ()))
    s1 = lax.dot_general(q[:, :B_QK_DIM], k[:, :B_QK_DIM], dn, preferred_element_type=F32) * scale + bias
    _online_step(s1, v, m1, l1, a1)
    s2 = lax.dot_general(q[:, B_QK_DIM:], k[:, B_QK_DIM:], dn, preferred_element_type=F32) * scale + bias
    _online_step(s2, v, m2, l2, a2)

    @pl.when(ki == pl.num_programs(3) - 1)
    def _():
        o = a1[...] / l1[...] - lam_ref[0] * (a2[...] / l2[...])
        o_ref[...] = _head_norm(o, gain_ref[...], head_scale).astype(BF16)


def _diff_attention(proj, row0, nseq, seq_t, lam, lam_init, gain_b):
    tq = _tile(seq_t, 256)
    tk = _tile(seq_t, 512)
    nq, nk = seq_t // tq, seq_t // tk
    slopes = jnp.asarray([2.0 ** (-8.0 * (h + 1) / B_HEADS) for h in range(B_HEADS)], F32)
    smem = pl.BlockSpec(memory_space=pltpu.SMEM)
    qcol, kcol, vcol = OFF_QB // HEAD_DIM, OFF_KB // HEAD_DIM, OFF_VB // HEAD_DIM
    return pl.pallas_call(
        functools.partial(_diff_attn_kernel, tq, tk, 1.0 - lam_init),
        out_shape=jax.ShapeDtypeStruct((nseq * seq_t, B_V_W), BF16),
        grid=(nseq, B_HEADS, nq, nk),
        in_specs=[smem, smem,
                  pl.BlockSpec((1, HEAD_DIM), lambda b, h, qi, ki: (0, h)),
                  pl.BlockSpec((tq, HEAD_DIM), lambda b, h, qi, ki: (row0 // tq + b * nq + qi, qcol + h)),
                  pl.BlockSpec((tk, HEAD_DIM), lambda b, h, qi, ki: (row0 // tk + b * nk + ki, kcol + h)),
                  pl.BlockSpec((tk, HEAD_DIM), lambda b, h, qi, ki: (row0 // tk + b * nk + ki, vcol + h))],
        out_specs=pl.BlockSpec((tq, HEAD_DIM), lambda b, h, qi, ki: (b * nq + qi, h)),
        scratch_shapes=[pltpu.VMEM((tq, 1), F32), pltpu.VMEM((tq, 1), F32), pltpu.VMEM((tq, HEAD_DIM), F32),
                        pltpu.VMEM((tq, 1), F32), pltpu.VMEM((tq, 1), F32), pltpu.VMEM((tq, HEAD_DIM), F32)],
        compiler_params=_cparams(("arbitrary", "arbitrary", "arbitrary", "arbitrary")),
        name="diff_attention",
    )(slopes, lam.reshape(1), gain_b, proj, proj, proj)


def _qk_prep_kernel(proj_ref, gain_ref, scale_ref, cos_ref, sin_ref, o_ref):
    j = pl.program_id(1)
    x = proj_ref[...].astype(F32)
    ms = jnp.mean(x * x, axis=-1, keepdims=True)
    y = x * lax.rsqrt(ms + RMS_EPS) * gain_ref[pl.ds(j, 1), :]
    lane = lax.broadcasted_iota(jnp.int32, y.shape, 1)
    partner = jnp.where((lane & 1) == 0, pltpu.roll(y, HEAD_DIM - 1, 1), pltpu.roll(y, 1, 1))
    o_ref[...] = ((y * cos_ref[...] + partner * sin_ref[...]) * scale_ref[pl.ds(j, 1), :]).astype(BF16)


def _qk_prep(proj, gains, scales, cos_t, sin_t):
    n = proj.shape[0]
    tr = _tile(n, 512)
    nh = C_HEADS + C_KV_HEADS
    c0 = OFF_QC // HEAD_DIM
    full = pl.BlockSpec((nh, HEAD_DIM), lambda i, j: (0, 0))
    tab = pl.BlockSpec((tr, HEAD_DIM), lambda i, j: (i, 0))
    return pl.pallas_call(
        _qk_prep_kernel,
        out_shape=jax.ShapeDtypeStruct((n, nh * HEAD_DIM), BF16),
        grid=(n // tr, nh),
        in_specs=[pl.BlockSpec((tr, HEAD_DIM), lambda i, j: (i, c0 + j)), full, full, tab, tab],
        out_specs=pl.BlockSpec((tr, HEAD_DIM), lambda i, j: (i, j)),
        compiler_params=_cparams(("arbitrary", "arbitrary")),
        name="qk_norm_rope",
    )(proj, gains, scales, cos_t, sin_t)


def _gqa_kernel(tq, gain_ref, q_ref, k_ref, v_ref, o_ref, m, l, acc):
    ki = pl.program_id(3)

    @pl.when(ki == 0)
    def _():
        _init_softmax(m, l, acc)

    q = jnp.concatenate([q_ref[:, g * HEAD_DIM:(g + 1) * HEAD_DIM] for g in range(C_GROUP)], axis=0)
    s = lax.dot_general(q, k_ref[...], (((1,), (1,)), ((), ())), preferred_element_type=F32)
    _online_step(s, v_ref[...], m, l, acc)

    @pl.when(ki == pl.num_programs(3) - 1)
    def _():
        o = acc[...] / l[...]
        for g in range(C_GROUP):
            gs = slice(g * HEAD_DIM, (g + 1) * HEAD_DIM)
            o_ref[:, gs] = _head_norm(o[g * tq:(g + 1) * tq], gain_ref[:, gs], 1.0).astype(BF16)


def _gqa_attention(qk_rot, proj, row0, nseq, seq_t, gain_c):
    tq = _tile(seq_t, 256)
    tk = _tile(seq_t, 512)
    nq, nk = seq_t // tq, seq_t // tk
    gw = C_GROUP * HEAD_DIM
    vcol = OFF_VC // HEAD_DIM
    return pl.pallas_call(
        functools.partial(_gqa_kernel, tq),
        out_shape=jax.ShapeDtypeStruct((nseq * seq_t, C_Q_W), BF16),
        grid=(nseq, C_KV_HEADS, nq, nk),
        in_specs=[pl.BlockSpec((1, gw), lambda b, h, qi, ki: (0, h)),
                  pl.BlockSpec((tq, gw), lambda b, h, qi, ki: (row0 // tq + b * nq + qi, h)),
                  pl.BlockSpec((tk, HEAD_DIM), lambda b, h, qi, ki: (row0 // tk + b * nk + ki, C_HEADS + h)),
                  pl.BlockSpec((tk, HEAD_DIM), lambda b, h, qi, ki: (row0 // tk + b * nk + ki, vcol + h))],
        out_specs=pl.BlockSpec((tq, gw), lambda b, h, qi, ki: (b * nq + qi, h)),
        scratch_shapes=[pltpu.VMEM((C_GROUP * tq, 1), F32), pltpu.VMEM((C_GROUP * tq, 1), F32),
                        pltpu.VMEM((C_GROUP * tq, HEAD_DIM), F32)],
        compiler_params=_cparams(("arbitrary", "arbitrary", "arbitrary", "arbitrary")),
        name="gqa_attention",
    )(gain_c, qk_rot, qk_rot, proj)


def _rope_tables(seq_t):
    rows = seq_t // GRID_W
    r = jnp.repeat(jnp.arange(rows, dtype=F32), GRID_W)
    c = jnp.tile(jnp.arange(GRID_W, dtype=F32), rows)
    half = HEAD_DIM // 2
    inv_freq = ROPE_THETA ** (-jnp.arange(0, half, 2, dtype=F32) / half)
    ang = jnp.concatenate([r[:, None] * inv_freq, c[:, None] * inv_freq], -1)
    cos = jnp.repeat(jnp.cos(ang), 2, axis=-1)
    sin = jnp.repeat(jnp.sin(ang), 2, axis=-1) * jnp.tile(jnp.asarray([-1.0, 1.0], F32), half)
    return cos, sin


def _first_argmax(v, lane, n):
    m = jnp.max(v, axis=-1, keepdims=True)
    idx = jnp.min(jnp.where(v == m, lane, float(n)), axis=-1, keepdims=True)
    return m, idx


def _router_kernel(n_exp, x_ref, wh_ref, wl_ref, bias_ref, idx_ref, wt_ref):
    x = x_ref[...]
    xh = x.astype(BF16)
    xl = (x - xh.astype(F32)).astype(BF16)
    logits = (jnp.dot(xh, wh_ref[...], preferred_element_type=F32)
              + jnp.dot(xh, wl_ref[...], preferred_element_type=F32)
              + jnp.dot(xl, wh_ref[...], preferred_element_type=F32))
    scores = jax.nn.sigmoid(logits)
    biased = scores + bias_ref[...]
    tr = x.shape[0]
    gsz = n_exp // N_GROUPS
    lane_i = lax.broadcasted_iota(jnp.int32, (tr, n_exp), 1)
    lane = lane_i.astype(F32)
    grp = (lane_i // gsz).astype(F32)
    neg = -jnp.inf
    gscore = jnp.zeros((tr, n_exp), F32)
    for g in range(N_GROUPS):
        in_g = grp == float(g)
        vg = jnp.where(in_g, biased, neg)
        m1, i1 = _first_argmax(vg, lane, n_exp)
        m2 = jnp.max(jnp.where(lane == i1, neg, vg), axis=-1, keepdims=True)
        gscore = jnp.where(in_g, m1 + m2, gscore)
    cand = jnp.full((tr, n_exp), neg, F32)
    for _ in range(TOPK_GROUPS):
        _, i = _first_argmax(gscore, lane, n_exp)
        pick = grp == jnp.floor(i * (1.0 / gsz))
        cand = jnp.where(pick, biased, cand)
        gscore = jnp.where(pick, neg, gscore)
    lane_o = lax.broadcasted_iota(jnp.int32, idx_ref.shape, 1)
    idx_out = jnp.zeros(idx_ref.shape, F32)
    wt_out = jnp.zeros(wt_ref.shape, F32)
    for kk in range(TOP_K):
        _, i = _first_argmax(cand, lane, n_exp)
        pick = lane == i
        sk = jnp.sum(jnp.where(pick, scores, 0.0), axis=-1, keepdims=True)
        cand = jnp.where(pick, neg, cand)
        idx_out = jnp.where(lane_o == kk, i, idx_out)
        wt_out = jnp.where(lane_o == kk, sk, wt_out)
    idx_ref[...] = idx_out.astype(jnp.int32)
    wt_ref[...] = wt_out /jnp.sum(wt_out, axis=-1, keepdims=True) * ROUTED_SCALE


def _router(x, router_w, router_bias):
    n, d = x.shape
    n_exp = router_w.shape[1]
    tr = _tile(n, 256)
    wh = router_w.astype(BF16)
    wl = (router_w - wh.astype(F32)).astype(BF16)
    full = pl.BlockSpec((d, n_exp), lambda i: (0, 0))
    out = pl.BlockSpec((tr, HEAD_DIM), lambda i: (i, 0))
    idx, wt = pl.pallas_call(
        functools.partial(_router_kernel, n_exp),
        out_shape=(jax.ShapeDtypeStruct((n, HEAD_DIM), jnp.int32), jax.ShapeDtypeStruct((n, HEAD_DIM), F32)),
        grid=(n // tr,),
        in_specs=[pl.BlockSpec((tr, d), lambda i: (i, 0)), full, full,
                  pl.BlockSpec((1, n_exp), lambda i: (0, 0))],
        out_specs=(out, out),
        compiler_params=_cparams(("arbitrary",)),
        name="moe_router",
    )(x, wh, wl, router_bias.reshape(1, n_exp).astype(F32))
    return idx[:, :TOP_K], wt[:, :TOP_K]


def _gather_rows_kernel(bm, nused_ref, tok_ref, x_hbm, o_ref, sem):
    b = pl.program_id(0)

    @pl.when(b < nused_ref[0])
    def _():
        def copy(r):
            return pltpu.make_async_copy(x_hbm.at[pl.ds(tok_ref[0, 0, r], 1), :],
                                         o_ref.at[pl.ds(r, 1), :], sem)

        def start(r, c):
            copy(r).start()
            return c

        def wait(r, c):
            copy(r).wait()
            return c

        lax.fori_loop(0, bm, start, 0)
        lax.fori_loop(0, bm, wait, 0)

    @pl.when(b >= nused_ref[0])
    def _():
        o_ref[...] = jnp.zeros(o_ref.shape, o_ref.dtype)


def _gather_rows(x, row_tok, n_used, bm):
    n, d = x.shape
    n_blocks = row_tok.shape[0] // bm

    def oidx(b, nused):
        return (b, 0)

    return pl.pallas_call(
        functools.partial(_gather_rows_kernel, bm),
        out_shape=jax.ShapeDtypeStruct((n_blocks * bm, d), x.dtype),
        grid_spec=pltpu.PrefetchScalarGridSpec(
            num_scalar_prefetch=1,
            grid=(n_blocks,),
            in_specs=[pl.BlockSpec((1, 1, bm), lambda b, nused: (b, 0, 0), memory_space=pltpu.SMEM),
                      pl.BlockSpec(memory_space=pl.ANY)],
            out_specs=pl.BlockSpec((bm, d), oidx),
            scratch_shapes=[pltpu.SemaphoreType.DMA],
        ),
        compiler_params=_cparams(("arbitrary",)),
        name="moe_dispatch_gather",
    )(n_used, row_tok.reshape(n_blocks, 1, bm), x)


def _expert_ffn_kernel(be_ref, nused_ref, x_ref, wg_ref, wu_ref, wd_ref, rw_ref, o_ref, acc_ref):
    b = pl.program_id(0)
    c = pl.program_id(1)

    @pl.when(b < nused_ref[0])
    def _():
        x = x_ref[...].astype(BF16)
        g = jnp.dot(x, wg_ref[0], preferred_element_type=F32)
        u = jnp.dot(x, wu_ref[0], preferred_element_type=F32)
        hmid = (g * jax.nn.sigmoid(g) * u).astype(BF16)
        part = jnp.dot(hmid, wd_ref[0], preferred_element_type=F32)

        @pl.when(c == 0)
        def _():
            acc_ref[...] = part

        @pl.when(c > 0)
        def _():
            acc_ref[...] += part

        @pl.when(c == pl.num_programs(1) - 1)
        def _():
            o_ref[...] = (acc_ref[...] * rw_ref[...]).astype(o_ref.dtype)

    @pl.when((b >= nused_ref[0]) & (c == pl.num_programs(1) - 1))
    def _():
        o_ref[...] = jnp.zeros(o_ref.shape, o_ref.dtype)


def _expert_ffn(x_rows, w_gate, w_up, w_down, row_w, block_e, n_used, bm, out_dtype):
    rows, d = x_rows.shape
    ff = w_gate.shape[-1]
    n_blocks = rows // bm
    fc = _tile(ff, 384) if ff % 384 == 0 else _tile(ff, 128)
    nfc = ff // fc

    def xidx(b, c, be, nused):
        return (jnp.minimum(b, nused[0] - 1), 0)

    def clampc(b, c, nused):
        return jnp.where(b < nused[0], c, nfc - 1)

    return pl.pallas_call(
        _expert_ffn_kernel,
        out_shape=jax.ShapeDtypeStruct((rows, d), out_dtype),
        grid_spec=pltpu.PrefetchScalarGridSpec(
            num_scalar_prefetch=2,
            grid=(n_blocks, nfc),
            in_specs=[pl.BlockSpec((bm, d), xidx),
                      pl.BlockSpec((1, d, fc), lambda b, c, be, nused: (be[b], 0, clampc(b, c, nused))),
                      pl.BlockSpec((1, d, fc), lambda b, c, be, nused: (be[b], 0, clampc(b, c, nused))),
                      pl.BlockSpec((1, fc, d), lambda b, c, be, nused: (be[b], clampc(b, c, nused), 0)),
                      pl.BlockSpec((bm, 1), xidx)],
            out_specs=pl.BlockSpec((bm, d), lambda b, c, be, nused: (b, 0)),
            scratch_shapes=[pltpu.VMEM((bm, d), F32)],
        ),
        compiler_params=_cparams(("arbitrary", "arbitrary")),
        name="expert_swiglu",
    )(block_e, n_used, x_rows, w_gate, w_up, w_down, row_w.reshape(rows, 1))


def _combine_ln_kernel(tr, alpha, pos_ref, x_ref, sh_ref, g_ref, b_ref, y_hbm, of_ref, ob_ref, buf, sem):
    def copy(i):
        return pltpu.make_async_copy(y_hbm.at[pl.ds(pos_ref[0, 0, i], 1), :], buf.at[pl.ds(i, 1), :], sem)

    def start(i, c):
        copy(i).start()
        return c

    def wait(i, c):
        copy(i).wait()
        return c

    lax.fori_loop(0, tr * TOP_K, start, 0)
    lax.fori_loop(0, tr * TOP_K, wait, 0)
    x = alpha * x_ref[...] + sh_ref[...].astype(F32)
    for kk in range(TOP_K):
        x = x + buf[kk * tr:(kk + 1) * tr, :].astype(F32)
    mu = jnp.mean(x, axis=-1, keepdims=True)
    xc = x - mu
    var = jnp.mean(xc * xc, axis=-1, keepdims=True)
    y = xc * lax.rsqrt(var + LN_EPS) * g_ref[...] + b_ref[...]
    of_ref[...] = y
    ob_ref[...] = y.astype(BF16)


def _combine_ln(x, shared, y_rows, pos, alpha, g, b):
    n, d = x.shape
    tr = _tile(n, 64)
    nt = n // tr
    pos_t = pos.reshape(nt, tr, TOP_K).transpose(0, 2, 1).reshape(nt, 1, TOP_K * tr)
    row = pl.BlockSpec((tr, d), lambda i: (i, 0))
    vec = pl.BlockSpec((1, d), lambda i: (0, 0))
    return pl.pallas_call(
        functools.partial(_combine_ln_kernel, tr, alpha),
        out_shape=(jax.ShapeDtypeStruct((n, d), F32), jax.ShapeDtypeStruct((n, d), BF16)),
        grid=(nt,),
        in_specs=[pl.BlockSpec((1, 1, TOP_K * tr), lambda i: (i, 0, 0), memory_space=pltpu.SMEM),
                  row, row, vec, vec, pl.BlockSpec(memory_space=pl.ANY)],
        out_specs=(row, row),
        scratch_shapes=[pltpu.VMEM((TOP_K * tr, d), y_rows.dtype), pltpu.SemaphoreType.DMA],
        compiler_params=_cparams(("arbitrary",)),
        name="moe_combine_layer_norm",
    )(pos_t, x, shared, g.reshape(1, d), b.reshape(1, d), y_rows)


def _dispatch_plan(idx, wts, n_exp, bm):
    n = idx.shape[0]
    nk = n * TOP_K
    n_blocks = -(-nk // bm) + n_exp
    rows = n_blocks * bm
    flat_e = idx.reshape(-1)
    order = jnp.argsort(flat_e, stable=True).astype(jnp.int32)
    e_s = flat_e[order]
    counts = jnp.zeros((n_exp,), jnp.int32).at[flat_e].add(1)
    padded = (counts + bm - 1) // bm * bm
    starts = jnp.cumsum(counts) - counts
    pends = jnp.cumsum(padded)
    pstarts = pends - padded
    dest = pstarts[e_s] + jnp.arange(nk, dtype=jnp.int32) - starts[e_s]
    row_tok = jnp.zeros((rows,), jnp.int32).at[dest].set(order // TOP_K)
    row_w = jnp.zeros((rows,), F32).at[dest].set(wts.reshape(-1)[order])
    pos = jnp.zeros((nk,), jnp.int32).at[order].set(dest).reshape(n, TOP_K)
    block_e = jnp.minimum(jnp.searchsorted(pends, jnp.arange(n_blocks, dtype=jnp.int32) * bm, side='right'),
                          n_exp - 1).astype(jnp.int32)
    n_used = (pends[-1] // bm).astype(jnp.int32).reshape(1)
    return row_tok, row_w, pos, block_e, n_used


def _token_mixer(xb, seqs, lam_init, w_in_l, w_out_l, out_gain_l, qk_gains, qk_scales, lam, cos_t, sin_t):
    proj = _matmul([xb], w_in_l, BF16)
    qk_rot = _qk_prep(proj, qk_gains, qk_scales, cos_t, sin_t)
    gain = out_gain_l.reshape(1, MIX_WIDTH).astype(F32)
    gain_a, gain_b, gain_c = gain[:, :A_W], gain[:, A_W:A_W + B_V_W], gain[:, A_W + B_V_W:]
    a_parts, b_parts, c_parts = [], [], []
    for row0, nseq, seq_t in seqs:
        outs, lses = [], []
        for g in range(len(DIL_GROUPS)):
            o, lse = _window_attention(proj, row0, nseq, seq_t, g)
            outs.append(o)
            lses.append(lse)
        a_parts.append(_window_merge(outs, lses, gain_a))
        b_parts.append(_diff_attention(proj, row0, nseq, seq_t, lam, lam_init, gain_b))
        c_parts.append(_gqa_attention(qk_rot, proj, row0, nseq, seq_t, gain_c))
    parts = [jnp.concatenate(p, axis=0) for p in (a_parts, b_parts, c_parts)]
    return _matmul(parts, w_out_l, F32)


def _moe(xf, xb, router_w, router_bias, w_gate, w_up, w_down, sh_gate, sh_up, sh_down, bm):
    n, d = xf.shape
    n_exp = router_w.shape[1]
    idx, wts = _router(xf, router_w, router_bias)
    row_tok, row_w, pos, block_e, n_used = _dispatch_plan(idx, wts, n_exp, bm)
    x_rows = _gather_rows(xf, row_tok, n_used, bm)
    y_rows = _expert_ffn(x_rows, w_gate, w_up, w_down, row_w, block_e, n_used, bm, F32)
    sb = _tile(n, bm)
    shared = _expert_ffn(xb, sh_gate[None], sh_up[None], sh_down[None], jnp.ones((n,), F32),
                         jnp.zeros((n // sb,), jnp.int32), jnp.full((1,), n // sb, jnp.int32), sb, BF16)
    return y_rows, shared, pos


def _forward(x_prompt, x_sample, ln_in_g, ln_in_b, w_in, w_out, out_gain, q_norm, k_norm,
             lambda_q1, lambda_k1, lambda_q2, lambda_k2, ln1_g, ln1_b, router_w, router_bias,
             w_gate, w_up, w_down, shared_gate, shared_up, shared_down, ln2_g, ln2_b, moe_block=256):
    depth = w_in.shape[0]
    d = x_prompt.shape[-1]
    alpha = (2 * depth) ** 0.25
    bp, tp, _ = x_prompt.shape
    bs, ts, _ = x_sample.shape
    n_p = bp * tp
    seqs = ((0, bp, tp), (n_p, bs, ts))
    x = jnp.concatenate([x_prompt.reshape(n_p, d), x_sample.reshape(bs * ts, d)], axis=0)

    tabs = [_rope_tables(t) for t in (tp, ts)]
    cos_t = jnp.concatenate([jnp.tile(tabs[0][0], (bp, 1)), jnp.tile(tabs[1][0], (bs, 1))], axis=0)
    sin_t = jnp.concatenate([jnp.tile(tabs[0][1], (bp, 1)), jnp.tile(tabs[1][1], (bs, 1))], axis=0)
    qk_scales = jnp.concatenate([jnp.full((C_HEADS, HEAD_DIM), HEAD_DIM ** -0.5, F32),
                                 jnp.ones((C_KV_HEADS, HEAD_DIM), F32)], axis=0)

    xf, xb = _layer_norm(x, [], 1.0, ln_in_g, ln_in_b)
    for l in range(depth):
        lam_init = 0.8 - 0.6 * math.exp(-0.3 * l)
        lam = (jnp.exp(jnp.sum(lambda_q1[l].astype(F32) * lambda_k1[l].astype(F32)))
               - jnp.exp(jnp.sum(lambda_q2[l].astype(F32) * lambda_k2[l].astype(F32))) + lam_init)
        qk_gains = jnp.concatenate([jnp.tile(q_norm[l][None].astype(F32), (C_HEADS, 1)),
                                    jnp.tile(k_norm[l][None].astype(F32), (C_KV_HEADS, 1))], axis=0)
        mix = _token_mixer(xb, seqs, lam_init, w_in[l].astype(BF16), w_out[l].astype(BF16), out_gain[l],
                           qk_gains, qk_scales, lam, cos_t, sin_t)
        xf, xb = _layer_norm(xf, [mix], alpha, ln1_g[l], ln1_b[l])
        y_rows, shared, pos = _moe(xf, xb, router_w[l], router_bias[l],
                                   w_gate[l].astype(BF16), w_up[l].astype(BF16), w_down[l].astype(BF16),
                                   shared_gate[l].astype(BF16), shared_up[l].astype(BF16),
                                   shared_down[l].astype(BF16), moe_block)
        xf, xb = _combine_ln(xf, shared, y_rows, pos, alpha, ln2_g[l], ln2_b[l])
    return xf[:n_p].reshape(bp, tp, d), xf[n_p:].reshape(bs, ts, d)


def kernel(x_prompt, x_sample, ln_in_g, ln_in_b, w_in, w_out, out_gain, q_norm, k_norm, lambda_q1, lambda_k1,
           lambda_q2, lambda_k2, ln1_g, ln1_b, router_w, router_bias, w_gate, w_up, w_down, shared_gate,
           shared_up, shared_down, ln2_g, ln2_b):
    return _forward(x_prompt, x_sample, ln_in_g, ln_in_b, w_in, w_out, out_gain, q_norm, k_norm,
                    lambda_q1, lambda_k1, lambda_q2, lambda_k2, ln1_g, ln1_b, router_w, router_bias,
                    w_gate, w_up, w_down, shared_gate, shared_up, shared_down, ln2_g, ln2_b)
```

```python
import functools
import math

import jax
import jax.numpy as jnp
from jax import lax
from jax.experimental import pallas as pl
from jax.experimental.pallas import tpu as pltpu

F32 = jnp.float32
BF16 = jnp.bfloat16

HEAD_DIM = 128
GRID_W = 64
DIL_GROUPS = ((128, 1), (512, 4), (2048, 16))
A_HEADS_PER_GROUP = 4
A_HEADS = 12
B_HEADS = 8
B_QK_DIM = 64
C_HEADS = 12
C_KV_HEADS = 4
C_GROUP = C_HEADS // C_KV_HEADS
ROPE_THETA = 10000.0
A_W = A_HEADS * HEAD_DIM
B_QK_W = B_HEADS * 2 * B_QK_DIM
B_V_W = B_HEADS * HEAD_DIM
C_Q_W = C_HEADS * HEAD_DIM
C_KV_W = C_KV_HEADS * HEAD_DIM
IN_WIDTH = 3 * A_W + 2 * B_QK_W + B_V_W + C_Q_W + 2 * C_KV_W
MIX_WIDTH = (A_HEADS + B_HEADS + C_HEADS) * HEAD_DIM
OFF_QA, OFF_KA, OFF_VA = 0, A_W, 2 * A_W
OFF_QB = 3 * A_W
OFF_KB = OFF_QB + B_QK_W
OFF_VB = OFF_KB + B_QK_W
OFF_QC = OFF_VB + B_V_W
OFF_KC = OFF_QC + C_Q_W
OFF_VC = OFF_KC + C_KV_W
TOP_K = 8
N_GROUPS = 8
TOPK_GROUPS = 4
ROUTED_SCALE = 2.5
LN_EPS = 1e-5
RMS_EPS = 1e-6
NEG_BIG = -1e30

V7X_VMEM_BYTES = 64 * 1024 * 1024
VMEM_LIMIT = 52 * 1024 * 1024


def _cparams(sem):
    return pltpu.CompilerParams(dimension_semantics=sem, vmem_limit_bytes=VMEM_LIMIT)


def _tile(n, pref):
    t = min(n, pref)
    while n % t:
        t //= 2
    return t


HI_HALF_MASK = -65536


def _pack_halves(lo, hi):
    lo_bits = lax.bitcast_convert_type(lo.astype(BF16).astype(F32), jnp.int32)
    hi_bits = lax.bitcast_convert_type(hi.astype(BF16).astype(F32), jnp.int32)
    return lax.shift_right_logical(lo_bits, 16) | (hi_bits & HI_HALF_MASK)


def _unpack_lo(w):
    return lax.bitcast_convert_type(w << 16, F32)


def _unpack_hi(w):
    return lax.bitcast_convert_type(w & HI_HALF_MASK, F32)


def _layer_norm_rows(x, g_ref, b_ref):
    mu = jnp.mean(x, axis=-1, keepdims=True)
    xc = x - mu
    var = jnp.mean(xc * xc, axis=-1, keepdims=True)
    return xc * lax.rsqrt(var + LN_EPS) * g_ref[...] + b_ref[...]


def _ln_kernel(alpha, n_add, packed, *refs):
    x_ref = refs[0]
    add_refs = refs[1:1 + n_add]
    g_ref, b_ref, of_ref, o2_ref = refs[1 + n_add:]
    x = x_ref[...]
    if n_add:
        x = alpha * x
        for a in add_refs:
            x = x + a[...].astype(F32)
    y = _layer_norm_rows(x, g_ref, b_ref)
    of_ref[...] = y
    if packed:
        half = y.shape[1] // 2
        o2_ref[...] = _pack_halves(y[:, :half], y[:, half:])
    else:
        o2_ref[...] = y.astype(BF16)


def _layer_norm(x, adds, alpha, g, b, packed=False):
    n, d = x.shape
    tr = _tile(n, 256)
    row = pl.BlockSpec((tr, d), lambda i: (i, 0))
    vec = pl.BlockSpec((1, d), lambda i: (0, 0))
    second = (jax.ShapeDtypeStruct((n, d // 2), jnp.int32) if packed else jax.ShapeDtypeStruct((n, d), BF16))
    return pl.pallas_call(
        functools.partial(_ln_kernel, alpha, len(adds), packed),
        out_shape=(jax.ShapeDtypeStruct((n, d), F32), second),
        grid=(n // tr,),
        in_specs=[row] * (1 + len(adds)) + [vec, vec],
        out_specs=(row, pl.BlockSpec((tr, second.shape[1]), lambda i: (i, 0))),
        compiler_params=_cparams(("arbitrary",)),
        name="layer_norm",
    )(x, *adds, g.reshape(1, d), b.reshape(1, d))


def _mm_kernel(ksizes, *refs):
    a_refs = refs[:len(ksizes)]
    w_ref, o_ref = refs[len(ksizes):]
    acc = None
    off = 0
    for a_ref, kk in zip(a_refs, ksizes):
        part = jnp.dot(a_ref[...], w_ref[off:off + kk, :], preferred_element_type=F32)
        acc = part if acc is None else acc + part
        off += kk
    o_ref[...] = acc.astype(o_ref.dtype)


def _matmul(parts, w, out_dtype):
    m = parts[0].shape[0]
    k, n = w.shape
    ksizes = tuple(p.shape[1] for p in parts)
    assert sum(ksizes) == k
    tm = _tile(m, 512)
    tn = _tile(n, 1024)
    in_specs = [pl.BlockSpec((tm, kk), lambda j, i: (i, 0)) for kk in ksizes]
    in_specs.append(pl.BlockSpec((k, tn), lambda j, i: (0, j)))
    return pl.pallas_call(
        functools.partial(_mm_kernel, ksizes),
        out_shape=jax.ShapeDtypeStruct((m, n), out_dtype),
        grid=(n // tn, m // tm),
        in_specs=in_specs,
        out_specs=pl.BlockSpec((tm, tn), lambda j, i: (i, j)),
        compiler_params=_cparams(("arbitrary", "arbitrary")),
        name="dense_projection",
    )(*parts, w)


def _head_norm(o, gain, head_scale):
    ms = jnp.mean(o * o, axis=-1, keepdims=True)
    return o * lax.rsqrt(ms + RMS_EPS) * (gain * head_scale)


def _window_attn_kernel(slopes, dil, seq_l, blk, half, q_ref, kp_ref, kc_ref, kn_ref,
                        vp_ref, vc_ref, vn_ref, o_ref, lse_ref):
    lb = pl.program_id(2)
    ii = lax.broadcasted_iota(jnp.int32, (blk, 3 * blk), 0)
    jj = lax.broadcasted_iota(jnp.int32, (blk, 3 * blk), 1)
    dist = jnp.abs(blk + ii - jj)
    kpos = (lb - 1) * blk + jj
    valid = (dist <= half) & (kpos >= 0) & (kpos < seq_l)
    tok_dist = (dist * dil).astype(F32)
    lane = lax.broadcasted_iota(jnp.int32, (blk, HEAD_DIM), 1)
    lse_tile = jnp.zeros((blk, HEAD_DIM), F32)
    scale = HEAD_DIM ** -0.5
    for h in range(A_HEADS_PER_GROUP):
        hs = slice(h * HEAD_DIM, (h + 1) * HEAD_DIM)
        q = q_ref[:, hs]
        k = jnp.concatenate([kp_ref[:, hs], kc_ref[:, hs], kn_ref[:, hs]], axis=0)
        v = jnp.concatenate([vp_ref[:, hs], vc_ref[:, hs], vn_ref[:, hs]], axis=0)
        s = lax.dot_general(q, k, (((1,), (1,)), ((), ())), preferred_element_type=F32) * scale
        s = jnp.where(valid, s - slopes[h] * tok_dist, NEG_BIG)
        m = jnp.max(s, axis=-1, keepdims=True)
        p = jnp.exp(s - m)
        den = jnp.sum(p, axis=-1, keepdims=True)
        o = jnp.dot(p.astype(BF16), v, preferred_element_type=F32)
        o_ref[:, hs] = o / den
        lse_tile = jnp.where(lane == h, m + jnp.log(den), lse_tile)
    lse_ref[...] = lse_tile


def _window_attention(proj, row0, nseq, seq_t, group):
    window, dil = DIL_GROUPS[group]
    half = (window // 2) // dil
    n_all = proj.shape[0]
    seq_l = seq_t // dil
    blk = _tile(seq_l, 128)
    nlb = seq_l // blk
    base = row0 // dil // blk
    cw = A_HEADS_PER_GROUP * HEAD_DIM
    cpr = IN_WIDTH // cw
    proj_v = proj.reshape(n_all // dil, dil * IN_WIDTH)
    slopes = tuple(2.0 ** (-8.0 * (group * A_HEADS_PER_GROUP + h + 1) / A_HEADS)
                   for h in range(A_HEADS_PER_GROUP))

    def spec(col0, shift):
        def imap(b, r, lb):
            nb = jnp.clip(lb + shift, 0, nlb - 1)
            return (base + b * nlb + nb, r * cpr + col0 // cw + group)
        return pl.BlockSpec((blk, cw), imap)

    rows = nseq * seq_l
    o, lse = pl.pallas_call(
        functools.partial(_window_attn_kernel, slopes, dil, seq_l, blk, half),
        out_shape=(jax.ShapeDtypeStruct((rows, dil * cw), F32),
                   jax.ShapeDtypeStruct((rows, dil * HEAD_DIM), F32)),
        grid=(nseq, dil, nlb),
        in_specs=[spec(OFF_QA, 0),
                  spec(OFF_KA, -1), spec(OFF_KA, 0), spec(OFF_KA, 1),
                  spec(OFF_VA, -1), spec(OFF_VA, 0), spec(OFF_VA, 1)],
        out_specs=(pl.BlockSpec((blk, cw), lambda b, r, lb: (b * nlb + lb, r)),
                   pl.BlockSpec((blk, HEAD_DIM), lambda b, r, lb: (b * nlb + lb, r))),
        compiler_params=_cparams(("arbitrary", "arbitrary", "arbitrary")),
        name="window_attention",
    )(proj_v, proj_v, proj_v, proj_v, proj_v, proj_v, proj_v)
    return o.reshape(nseq * seq_t, cw), lse.reshape(nseq * seq_t, HEAD_DIM)


def _window_merge_kernel(o0_ref, o1_ref, o2_ref, l0_ref, l1_ref, l2_ref, gain_ref, out_ref):
    o_refs = (o0_ref, o1_ref, o2_ref)
    lses = (l0_ref[...], l1_ref[...], l2_ref[...])
    mx = jnp.maximum(jnp.maximum(lses[0], lses[1]), lses[2])
    es = [jnp.exp(l - mx) for l in lses]
    tot = es[0] + es[1] + es[2]
    for g in range(len(DIL_GROUPS)):
        alpha = es[g] / tot
        for h in range(A_HEADS_PER_GROUP):
            hs = slice(h * HEAD_DIM, (h + 1) * HEAD_DIM)
            c0 = (g * A_HEADS_PER_GROUP + h) * HEAD_DIM
            o = o_refs[g][:, hs] * alpha[:, h:h + 1]
            out_ref[:, c0:c0 + HEAD_DIM] = _head_norm(o, gain_ref[:, c0:c0 + HEAD_DIM], 1.0).astype(BF16)


def _window_merge(outs, lses, gain_a):
    n = outs[0].shape[0]
    cw = A_HEADS_PER_GROUP * HEAD_DIM
    tr = _tile(n, 256)
    return pl.pallas_call(
        _window_merge_kernel,
        out_shape=jax.ShapeDtypeStruct((n, A_W), BF16),
        grid=(n // tr,),
        in_specs=[pl.BlockSpec((tr, cw), lambda i: (i, 0))] * 3
        + [pl.BlockSpec((tr, HEAD_DIM), lambda i: (i, 0))] * 3
        + [pl.BlockSpec((1, A_W), lambda i: (0, 0))],
        out_specs=pl.BlockSpec((tr, A_W), lambda i: (i, 0)),
        compiler_params=_cparams(("arbitrary",)),
        name="window_merge",
    )(*outs, *lses, gain_a)


LOG2E = math.log2(math.e)
ATTN_KV_CHUNK = 1024
DIFF_Q_BLOCK = 512
GQA_Q_BLOCK = 256


def _flash_loop(q, k_ref, v_ref, sa_ref, sb_ref, m_ref, acc_ref, tkc, parts, bias_fn):
    n_chunks = k_ref.shape[0] // tkc
    ones = jnp.ones((tkc, HEAD_DIM), BF16)
    m_ref[...] = jnp.full(m_ref.shape, -jnp.inf, F32)
    acc_ref[...] = jnp.zeros(acc_ref.shape, F32)

    def scores(c):
        ks = pl.ds(pl.multiple_of(c * tkc, tkc), tkc)
        return lax.dot_general(q, k_ref[ks, :], (((1,), (1,)), ((), ())), preferred_element_type=F32)

    def consume(s_ref, c):
        ks = pl.ds(pl.multiple_of(c * tkc, tkc), tkc)
        v_ext = jnp.concatenate([v_ref[ks, :], ones], axis=1)
        for r0, r1 in parts:
            u = bias_fn(c, s_ref[r0:r1, :])
            m_prev = m_ref[r0:r1, :]
            m_new = jnp.maximum(m_prev, jnp.max(u, axis=-1, keepdims=True))
            p = jnp.exp2((u - m_new).astype(BF16))
            alpha = jnp.exp2(m_prev - m_new)
            acc_ref[r0:r1, :] = alpha * acc_ref[r0:r1, :] + jnp.dot(p, v_ext, preferred_element_type=F32)
            m_ref[r0:r1, :] = m_new

    sa_ref[...] = scores(0)

    def body(i, carry):
        c = 2 * i
        sb_ref[...] = scores(c + 1)
        consume(sa_ref, c)
        sa_ref[...] = scores(jnp.minimum(c + 2, n_chunks - 1))
        consume(sb_ref, c + 1)
        return carry

    lax.fori_loop(0, n_chunks // 2, body, 0)


def _softmax_out(acc_ref, rows):
    acc = acc_ref[rows, :]
    return acc[:, :HEAD_DIM] / acc[:, HEAD_DIM:]


def _diff_attn_kernel(tq, tkc, head_scale, slope_ref, lam_ref, gain_ref, q_ref, k_ref, v_ref, o_ref,
                      sa_ref, sb_ref, m_ref, acc_ref):
    q0 = pl.program_id(2) * tq
    q = q_ref[...]
    lane = lax.broadcasted_iota(jnp.int32, q.shape, 1)
    zero = jnp.zeros_like(q)
    q = jnp.concatenate([jnp.where(lane < B_QK_DIM, q, zero), jnp.where(lane >= B_QK_DIM, q, zero)], axis=0)
    slope2 = slope_ref[pl.program_id(1)]
    dij = (lax.broadcasted_iota(jnp.int32, (tq, tkc), 0)
           - lax.broadcasted_iota(jnp.int32, (tq, tkc), 1)).astype(F32)

    def bias_fn(c, s):
        d = (q0 - c * tkc).astype(F32)
        return s - slope2 * jnp.abs(dij + d)

    _flash_loop(q, k_ref, v_ref, sa_ref, sb_ref, m_ref, acc_ref, tkc, ((0, tq), (tq, 2 * tq)), bias_fn)
    o = _softmax_out(acc_ref, slice(0, tq)) - lam_ref[0] * _softmax_out(acc_ref, slice(tq, 2 * tq))
    o_ref[...] = _head_norm(o, gain_ref[...], head_scale).astype(BF16)


def _kv_chunk(seq_t):
    return _tile(seq_t // 2, ATTN_KV_CHUNK)


def _diff_attention(proj, row0, nseq, seq_t, lam, lam_init, gain_b):
    tq = _tile(seq_t, DIFF_Q_BLOCK)
    tkc = _kv_chunk(seq_t)
    nq = seq_t // tq
    slopes2 = jnp.asarray([2.0 ** (-8.0 * (h + 1) / B_HEADS) * LOG2E for h in range(B_HEADS)], F32)
    smem = pl.BlockSpec(memory_space=pltpu.SMEM)
    qcol, kcol, vcol = OFF_QB // HEAD_DIM, OFF_KB // HEAD_DIM, OFF_VB // HEAD_DIM
    return pl.pallas_call(
        functools.partial(_diff_attn_kernel, tq, tkc, 1.0 - lam_init),
        out_shape=jax.ShapeDtypeStruct((nseq * seq_t, B_V_W), BF16),
        grid=(nseq, B_HEADS, nq),
        in_specs=[smem, smem,
                  pl.BlockSpec((1, HEAD_DIM), lambda b, h, qi: (0, h)),
                  pl.BlockSpec((tq, HEAD_DIM), lambda b, h, qi: (row0 // tq + b * nq + qi, qcol + h)),
                  pl.BlockSpec((seq_t, HEAD_DIM), lambda b, h, qi: (row0 // seq_t + b, kcol + h)),
                  pl.BlockSpec((seq_t, HEAD_DIM), lambda b, h, qi: (row0 // seq_t + b, vcol + h))],
        out_specs=pl.BlockSpec((tq, HEAD_DIM), lambda b, h, qi: (b * nq + qi, h)),
        scratch_shapes=[pltpu.VMEM((2 * tq, tkc), F32), pltpu.VMEM((2 * tq, tkc), F32),
                        pltpu.VMEM((2 * tq, 1), F32), pltpu.VMEM((2 * tq, 2 * HEAD_DIM), F32)],
        compiler_params=_cparams(("arbitrary", "arbitrary", "arbitrary")),
        name="diff_attention",
    )(slopes2, lam.reshape(1), gain_b, proj, proj, proj)


def _qk_prep_kernel(proj_ref, gain_ref, scale_ref, cos_ref, sin_ref, o_ref):
    j = pl.program_id(1)
    x = proj_ref[...].astype(F32)
    ms = jnp.mean(x * x, axis=-1, keepdims=True)
    y = x * lax.rsqrt(ms + RMS_EPS) * gain_ref[pl.ds(j, 1), :]
    lane = lax.broadcasted_iota(jnp.int32, y.shape, 1)
    partner = jnp.where((lane & 1) == 0, pltpu.roll(y, HEAD_DIM - 1, 1), pltpu.roll(y, 1, 1))
    o_ref[...] = ((y * cos_ref[...] + partner * sin_ref[...]) * scale_ref[pl.ds(j, 1), :]).astype(BF16)


def _qk_prep(proj, gains, scales, cos_t, sin_t):
    n = proj.shape[0]
    tr = _tile(n, 512)
    nh = C_HEADS + C_KV_HEADS
    c0 = OFF_QC // HEAD_DIM
    full = pl.BlockSpec((nh, HEAD_DIM), lambda i, j: (0, 0))
    tab = pl.BlockSpec((tr, HEAD_DIM), lambda i, j: (i, 0))
    return pl.pallas_call(
        _qk_prep_kernel,
        out_shape=jax.ShapeDtypeStruct((n, nh * HEAD_DIM), BF16),
        grid=(n // tr, nh),
        in_specs=[pl.BlockSpec((tr, HEAD_DIM), lambda i, j: (i, c0 + j)), full, full, tab, tab],
        out_specs=pl.BlockSpec((tr, HEAD_DIM), lambda i, j: (i, j)),
        compiler_params=_cparams(("arbitrary", "arbitrary")),
        name="qk_norm_rope",
    )(proj, gains, scales, cos_t, sin_t)


def _gqa_kernel(tq, tkc, gain_ref, q_ref, k_ref, v_ref, o_ref, sa_ref, sb_ref, m_ref, acc_ref):
    q = jnp.concatenate([q_ref[:, g * HEAD_DIM:(g + 1) * HEAD_DIM] for g in range(C_GROUP)], axis=0)
    _flash_loop(q, k_ref, v_ref, sa_ref, sb_ref, m_ref, acc_ref, tkc, ((0, C_GROUP * tq),), lambda c, s: s)
    for g in range(C_GROUP):
        gs = slice(g * HEAD_DIM, (g + 1) * HEAD_DIM)
        o = _softmax_out(acc_ref, slice(g * tq, (g + 1) * tq))
        o_ref[:, gs] = _head_norm(o, gain_ref[:, gs], 1.0).astype(BF16)


def _gqa_attention(qk_rot, proj, row0, nseq, seq_t, gain_c):
    tq = _tile(seq_t, GQA_Q_BLOCK)
    tkc = _kv_chunk(seq_t)
    nq = seq_t // tq
    gw = C_GROUP * HEAD_DIM
    vcol = OFF_VC // HEAD_DIM
    rows = C_GROUP * tq
    return pl.pallas_call(
        functools.partial(_gqa_kernel, tq, tkc),
        out_shape=jax.ShapeDtypeStruct((nseq * seq_t, C_Q_W), BF16),
        grid=(nseq, C_KV_HEADS, nq),
        in_specs=[pl.BlockSpec((1, gw), lambda b, h, qi: (0, h)),
                  pl.BlockSpec((tq, gw), lambda b, h, qi: (row0 // tq + b * nq + qi, h)),
                  pl.BlockSpec((seq_t, HEAD_DIM), lambda b, h, qi: (row0 // seq_t + b, C_HEADS + h)),
                  pl.BlockSpec((seq_t, HEAD_DIM), lambda b, h, qi: (row0 // seq_t + b, vcol + h))],
        out_specs=pl.BlockSpec((tq, gw), lambda b, h, qi: (b * nq + qi, h)),
        scratch_shapes=[pltpu.VMEM((rows, tkc), F32), pltpu.VMEM((rows, tkc), F32),
                        pltpu.VMEM((rows, 1), F32), pltpu.VMEM((rows, 2 * HEAD_DIM), F32)],
        compiler_params=_cparams(("arbitrary", "arbitrary", "arbitrary")),
        name="gqa_attention",
    )(gain_c, qk_rot, qk_rot, proj)


def _rope_tables(seq_t):
    rows = seq_t // GRID_W
    r = jnp.repeat(jnp.arange(rows, dtype=F32), GRID_W)
    c = jnp.tile(jnp.arange(GRID_W, dtype=F32), rows)
    half = HEAD_DIM // 2
    inv_freq = ROPE_THETA ** (-jnp.arange(0, half, 2, dtype=F32) / half)
    ang = jnp.concatenate([r[:, None] * inv_freq, c[:, None] * inv_freq], -1)
    cos = jnp.repeat(jnp.cos(ang), 2, axis=-1)
    sin = jnp.repeat(jnp.sin(ang), 2, axis=-1) * jnp.tile(jnp.asarray([-1.0, 1.0], F32), half)
    return cos, sin


def _first_argmax(v, lane, n):
    m = jnp.max(v, axis=-1, keepdims=True)
    idx = jnp.min(jnp.where(v == m, lane, float(n)), axis=-1, keepdims=True)
    return m, idx


def _router_kernel(n_exp, x_ref, wh_ref, wl_ref, bias_ref, idx_ref, wt_ref):
    x = x_ref[...]
    xh = x.astype(BF16)
    xl = (x - xh.astype(F32)).astype(BF16)
    logits = (jnp.dot(xh, wh_ref[...], preferred_element_type=F32)
              + jnp.dot(xh, wl_ref[...], preferred_element_type=F32)
              + jnp.dot(xl, wh_ref[...], preferred_element_type=F32))
    scores = jax.nn.sigmoid(logits)
    biased = scores + bias_ref[...]
    tr = x.shape[0]
    gsz = n_exp // N_GROUPS
    lane_i = lax.broadcasted_iota(jnp.int32, (tr, n_exp), 1)
    lane = lane_i.astype(F32)
    grp = (lane_i // gsz).astype(F32)
    neg = -jnp.inf
    gscore = jnp.zeros((tr, n_exp), F32)
    for g in range(N_GROUPS):
        in_g = grp == float(g)
        vg = jnp.where(in_g, biased, neg)
        m1, i1 = _first_argmax(vg, lane, n_exp)
        m2 = jnp.max(jnp.where(lane == i1, neg, vg), axis=-1, keepdims=True)
        gscore = jnp.where(in_g, m1 + m2, gscore)
    cand = jnp.full((tr, n_exp), neg, F32)
    for _ in range(TOPK_GROUPS):
        _, i = _first_argmax(gscore, lane, n_exp)
        pick = grp == jnp.floor(i * (1.0 / gsz))
        cand = jnp.where(pick, biased, cand)
        gscore = jnp.where(pick, neg, gscore)
    lane_o = lax.broadcasted_iota(jnp.int32, idx_ref.shape, 1)
    idx_out = jnp.zeros(idx_ref.shape, F32)
    wt_out = jnp.zeros(wt_ref.shape, F32)
    for kk in range(TOP_K):
        _, i = _first_argmax(cand, lane, n_exp)
        pick = lane == i
        sk = jnp.sum(jnp.where(pick, scores, 0.0), axis=-1, keepdims=True)
        cand = jnp.where(pick, neg, cand)
        idx_out = jnp.where(lane_o == kk, i, idx_out)
        wt_out = jnp.where(lane_o == kk, sk, wt_out)
    idx_ref[...] = idx_out.astype(jnp.int32)
    wt_ref[...] = wt_out /jnp.sum(wt_out, axis=-1, keepdims=True) * ROUTED_SCALE


def _router(x, router_w, router_bias):
    n, d = x.shape
    n_exp = router_w.shape[1]
    tr = _tile(n, 256)
    wh = router_w.astype(BF16)
    wl = (router_w - wh.astype(F32)).astype(BF16)
    full = pl.BlockSpec((d, n_exp), lambda i: (0, 0))
    out = pl.BlockSpec((tr, HEAD_DIM), lambda i: (i, 0))
    idx, wt = pl.pallas_call(
        functools.partial(_router_kernel, n_exp),
        out_shape=(jax.ShapeDtypeStruct((n, HEAD_DIM), jnp.int32), jax.ShapeDtypeStruct((n, HEAD_DIM), F32)),
        grid=(n // tr,),
        in_specs=[pl.BlockSpec((tr, d), lambda i: (i, 0)), full, full,
                  pl.BlockSpec((1, n_exp), lambda i: (0, 0))],
        out_specs=(out, out),
        compiler_params=_cparams(("arbitrary",)),
        name="moe_router",
    )(x, wh, wl, router_bias.reshape(1, n_exp).astype(F32))
    return idx[:, :TOP_K], wt[:, :TOP_K]


ROW_DMA_UNROLL = 8


def _row_dma_burst(n_rows, copy):
    assert n_rows % ROW_DMA_UNROLL == 0

    def start(i, c):
        for u in range(ROW_DMA_UNROLL):
            copy(i * ROW_DMA_UNROLL + u).start(priority=u % 2)
        return c

    def wait(i, c):
        for u in range(ROW_DMA_UNROLL):
            copy(i * ROW_DMA_UNROLL + u).wait()
        return c

    lax.fori_loop(0, n_rows // ROW_DMA_UNROLL, start, 0)
    lax.fori_loop(0, n_rows // ROW_DMA_UNROLL, wait, 0)


def _gather_rows_kernel(bm, nused_ref, tok_ref, x_hbm, o_ref, sem):
    b = pl.program_id(0)

    @pl.when(b < nused_ref[0])
    def _():
        def copy(r):
            return pltpu.make_async_copy(x_hbm.at[pl.ds(tok_ref[0, 0, r], 1), :],
                                         o_ref.at[pl.ds(r, 1), :], sem)

        _row_dma_burst(bm, copy)

    @pl.when(b >= nused_ref[0])
    def _():
        o_ref[...] = jnp.zeros(o_ref.shape, o_ref.dtype)


def _gather_rows(x, row_tok, n_used, bm):
    n, d = x.shape
    n_blocks = row_tok.shape[0] // bm

    def oidx(b, nused):
        return (b, 0)

    return pl.pallas_call(
        functools.partial(_gather_rows_kernel, bm),
        out_shape=jax.ShapeDtypeStruct((n_blocks * bm, d), x.dtype),
        grid_spec=pltpu.PrefetchScalarGridSpec(
            num_scalar_prefetch=1,
            grid=(n_blocks,),
            in_specs=[pl.BlockSpec((1, 1, bm), lambda b, nused: (b, 0, 0), memory_space=pltpu.SMEM),
                      pl.BlockSpec(memory_space=pl.ANY)],
            out_specs=pl.BlockSpec((bm, d), oidx),
            scratch_shapes=[pltpu.SemaphoreType.DMA],
        ),
        compiler_params=_cparams(("arbitrary",)),
        name="moe_dispatch_gather",
    )(n_used, row_tok.reshape(n_blocks, 1, bm), x)


def _gate_up_kernel(be_ref, nused_ref, x_ref, wg_ref, wu_ref, h_ref, g_acc, u_acc):
    b = pl.program_id(0)
    kc = pl.program_id(1)

    @pl.when(b < nused_ref[0])
    def _():
        w = x_ref[...]
        x = lax.bitcast_convert_type(jnp.where(kc == 0, w << 16, w & HI_HALF_MASK), F32).astype(BF16)
        g = jnp.dot(x, wg_ref[0], preferred_element_type=F32)
        u = jnp.dot(x, wu_ref[0], preferred_element_type=F32)

        @pl.when(kc == 0)
        def _():
            g_acc[...] = g
            u_acc[...] = u

        @pl.when(kc == 1)
        def _():
            gg = g_acc[...] + g
            h_ref[...] = (gg * jax.nn.sigmoid(gg) * (u_acc[...] + u)).astype(BF16)

    @pl.when((b >= nused_ref[0]) & (kc == 1))
    def _():
        h_ref[...] = jnp.zeros(h_ref.shape, h_ref.dtype)


def _down_kernel(be_ref, nused_ref, h_ref, wlo_ref, whi_ref, rw_ref, o_ref):
    b = pl.program_id(0)

    @pl.when(b < nused_ref[0])
    def _():
        h = h_ref[...]
        rw = rw_ref[...]
        o_ref[...] = _pack_halves(jnp.dot(h, wlo_ref[0], preferred_element_type=F32) * rw,
                                  jnp.dot(h, whi_ref[0], preferred_element_type=F32) * rw)

    @pl.when(b >= nused_ref[0])
    def _():
        o_ref[...] = jnp.zeros(o_ref.shape, o_ref.dtype)


def _expert_ffn(x_rows, w_gate, w_up, w_down, row_w, block_e, n_used, bm):
    rows, dh = x_rows.shape
    ff = w_gate.shape[-1]
    n_blocks = rows // bm

    def live(b, nused):
        return jnp.minimum(b, nused[0] - 1)

    def kclamp(b, kc, nused):
        return jnp.where(b < nused[0], kc, 1)

    hmid = pl.pallas_call(
        _gate_up_kernel,
        out_shape=jax.ShapeDtypeStruct((rows, ff), BF16),
        grid_spec=pltpu.PrefetchScalarGridSpec(
            num_scalar_prefetch=2,
            grid=(n_blocks, 2),
            in_specs=[pl.BlockSpec((bm, dh), lambda b, kc, be, nused: (live(b, nused), 0)),
                      pl.BlockSpec((1, dh, ff), lambda b, kc, be, nused: (be[b], kclamp(b, kc, nused), 0)),
                      pl.BlockSpec((1, dh, ff), lambda b, kc, be, nused: (be[b], kclamp(b, kc, nused), 0))],
            out_specs=pl.BlockSpec((bm, ff), lambda b, kc, be, nused: (b, 0)),
            scratch_shapes=[pltpu.VMEM((bm, ff), F32), pltpu.VMEM((bm, ff), F32)],
        ),
        compiler_params=_cparams(("arbitrary", "arbitrary")),
        name="expert_gate_up",
    )(block_e, n_used, x_rows, w_gate, w_up)

    tn = _tile(dh, 1024)
    nn = dh // tn
    return pl.pallas_call(
        _down_kernel,
        out_shape=jax.ShapeDtypeStruct((rows, dh), jnp.int32),
        grid_spec=pltpu.PrefetchScalarGridSpec(
            num_scalar_prefetch=2,
            grid=(n_blocks, nn),
            in_specs=[pl.BlockSpec((bm, ff), lambda b, j, be, nused: (live(b, nused), 0)),
                      pl.BlockSpec((1, ff, tn), lambda b, j, be, nused: (be[b], 0, j)),
                      pl.BlockSpec((1, ff, tn), lambda b, j, be, nused: (be[b], 0, nn + j)),
                      pl.BlockSpec((bm, 1), lambda b, j, be, nused: (live(b, nused), 0))],
            out_specs=pl.BlockSpec((bm, tn), lambda b, j, be, nused: (b, j)),
        ),
        compiler_params=_cparams(("arbitrary", "arbitrary")),
        name="expert_down",
    )(block_e, n_used, hmid, w_down, w_down, row_w.reshape(rows, 1))


def _combine_ln_kernel(tr, alpha, pos_ref, x_ref, sh_ref, g_ref, b_ref, y_hbm, of_ref, ob_ref, buf, sem):
    def copy(i):
        return pltpu.make_async_copy(y_hbm.at[pl.ds(pos_ref[0, 0, i], 1), :], buf.at[pl.ds(i, 1), :], sem)

    _row_dma_burst(tr * TOP_K, copy)
    w = sh_ref[...]
    lo, hi = _unpack_lo(w), _unpack_hi(w)
    for kk in range(TOP_K):
        w = buf[kk * tr:(kk + 1) * tr, :]
        lo, hi = lo + _unpack_lo(w), hi + _unpack_hi(w)
    x = alpha * x_ref[...] + jnp.concatenate([lo, hi], axis=1)
    y = _layer_norm_rows(x, g_ref, b_ref)
    of_ref[...] = y
    ob_ref[...] = y.astype(BF16)


def _combine_ln(x, shared, y_rows, pos, alpha, g, b):
    n, d = x.shape
    tr = _tile(n, 64)
    nt = n // tr
    pos_t = pos.reshape(nt, tr, TOP_K).transpose(0, 2, 1).reshape(nt, 1, TOP_K * tr)
    row = pl.BlockSpec((tr, d), lambda i: (i, 0))
    half_row = pl.BlockSpec((tr, d // 2), lambda i: (i, 0))
    vec = pl.BlockSpec((1, d), lambda i: (0, 0))
    return pl.pallas_call(
        functools.partial(_combine_ln_kernel, tr, alpha),
        out_shape=(jax.ShapeDtypeStruct((n, d), F32), jax.ShapeDtypeStruct((n, d), BF16)),
        grid=(nt,),
        in_specs=[pl.BlockSpec((1, 1, TOP_K * tr), lambda i: (i, 0, 0), memory_space=pltpu.SMEM),
                  row, half_row, vec, vec, pl.BlockSpec(memory_space=pl.ANY)],
        out_specs=(row, row),
        scratch_shapes=[pltpu.VMEM((TOP_K * tr, d // 2), jnp.int32), pltpu.SemaphoreType.DMA],
        compiler_params=_cparams(("arbitrary",)),
        name="moe_combine_layer_norm",
    )(pos_t, x, shared, g.reshape(1, d), b.reshape(1, d), y_rows)


def _dispatch_plan(idx, wts, n_exp, bm):
    n = idx.shape[0]
    nk = n * TOP_K
    n_blocks = -(-nk // bm) + n_exp
    rows = n_blocks * bm
    flat_e = idx.reshape(-1)
    order = jnp.argsort(flat_e, stable=True).astype(jnp.int32)
    e_s = flat_e[order]
    counts = jnp.zeros((n_exp,), jnp.int32).at[flat_e].add(1)
    padded = (counts + bm - 1) // bm * bm
    starts = jnp.cumsum(counts) - counts
    pends = jnp.cumsum(padded)
    pstarts = pends - padded
    dest = pstarts[e_s] + jnp.arange(nk, dtype=jnp.int32) - starts[e_s]
    row_tok = jnp.zeros((rows,), jnp.int32).at[dest].set(order // TOP_K)
    row_w = jnp.zeros((rows,), F32).at[dest].set(wts.reshape(-1)[order])
    pos = jnp.zeros((nk,), jnp.int32).at[order].set(dest).reshape(n, TOP_K)
    block_e = jnp.minimum(jnp.searchsorted(pends, jnp.arange(n_blocks, dtype=jnp.int32) * bm, side='right'),
                          n_exp - 1).astype(jnp.int32)
    n_used = (pends[-1] // bm).astype(jnp.int32).reshape(1)
    return row_tok, row_w, pos, block_e, n_used


def _token_mixer(xb, seqs, lam_init, w_in_l, w_out_l, out_gain_l, qk_gains, qk_scales, lam, cos_t, sin_t):
    proj = _matmul([xb], w_in_l, BF16)
    qk_rot = _qk_prep(proj, qk_gains, qk_scales, cos_t, sin_t)
    gain = out_gain_l.reshape(1, MIX_WIDTH).astype(F32)
    gain_a, gain_b, gain_c = gain[:, :A_W], gain[:, A_W:A_W + B_V_W], gain[:, A_W + B_V_W:]
    a_parts, b_parts, c_parts = [], [], []
    for row0, nseq, seq_t in seqs:
        outs, lses = [], []
        for g in range(len(DIL_GROUPS)):
            o, lse = _window_attention(proj, row0, nseq, seq_t, g)
            outs.append(o)
            lses.append(lse)
        a_parts.append(_window_merge(outs, lses, gain_a))
        b_parts.append(_diff_attention(proj, row0, nseq, seq_t, lam, lam_init, gain_b))
        c_parts.append(_gqa_attention(qk_rot, proj, row0, nseq, seq_t, gain_c))
    parts = [jnp.concatenate(p, axis=0) for p in (a_parts, b_parts, c_parts)]
    return _matmul(parts, w_out_l, F32)


def _moe(xf, xp, router_w, router_bias, w_gate, w_up, w_down, sh_gate, sh_up, sh_down, bm):
    n = xf.shape[0]
    n_exp = router_w.shape[1]
    idx, wts = _router(xf, router_w, router_bias)
    row_tok, row_w, pos, block_e, n_used = _dispatch_plan(idx, wts, n_exp, bm)
    x_rows = _gather_rows(xp, row_tok, n_used, bm)
    y_rows = _expert_ffn(x_rows, w_gate, w_up, w_down, row_w, block_e, n_used, bm)
    sb = _tile(n, bm)
    shared = _expert_ffn(xp, sh_gate[None], sh_up[None], sh_down[None], jnp.ones((n,), F32),
                         jnp.zeros((n // sb,), jnp.int32), jnp.full((1,), n // sb, jnp.int32), sb)
    return y_rows, shared, pos


def _forward(x_prompt, x_sample, ln_in_g, ln_in_b, w_in, w_out, out_gain, q_norm, k_norm,
             lambda_q1, lambda_k1, lambda_q2, lambda_k2, ln1_g, ln1_b, router_w, router_bias,
             w_gate, w_up, w_down, shared_gate, shared_up, shared_down, ln2_g, ln2_b, moe_block=512):
    depth = w_in.shape[0]
    d = x_prompt.shape[-1]
    alpha = (2 * depth) ** 0.25
    bp, tp, _ = x_prompt.shape
    bs, ts, _ = x_sample.shape
    n_s = bs * ts
    assert ts >= tp and n_s % tp == 0
    seqs = ((0, bs, ts), (n_s, bp, tp))
    x = jnp.concatenate([x_sample.reshape(n_s, d), x_prompt.reshape(bp * tp, d)], axis=0)

    tabs = [_rope_tables(t) for t in (ts, tp)]
    cos_t = jnp.concatenate([jnp.tile(tabs[0][0], (bs, 1)), jnp.tile(tabs[1][0], (bp, 1))], axis=0)
    sin_t = jnp.concatenate([jnp.tile(tabs[0][1], (bs, 1)), jnp.tile(tabs[1][1], (bp, 1))], axis=0)
    qk_scales = jnp.concatenate([jnp.full((C_HEADS, HEAD_DIM), HEAD_DIM ** -0.5 * LOG2E, F32),
                                 jnp.ones((C_KV_HEADS, HEAD_DIM), F32)], axis=0)
    in_col_scale = jnp.ones((IN_WIDTH,), F32).at[OFF_QB:OFF_KB].set(B_QK_DIM ** -0.5 * LOG2E)

    xf, xb = _layer_norm(x, [], 1.0, ln_in_g, ln_in_b)
    for l in range(depth):
        lam_init = 0.8 - 0.6 * math.exp(-0.3 * l)
        lam = (jnp.exp(jnp.sum(lambda_q1[l].astype(F32) * lambda_k1[l].astype(F32)))
               - jnp.exp(jnp.sum(lambda_q2[l].astype(F32) * lambda_k2[l].astype(F32))) + lam_init)
        qk_gains = jnp.concatenate([jnp.tile(q_norm[l][None].astype(F32), (C_HEADS, 1)),
                                    jnp.tile(k_norm[l][None].astype(F32), (C_KV_HEADS, 1))], axis=0)
        mix = _token_mixer(xb, seqs, lam_init, (w_in[l] * in_col_scale).astype(BF16), w_out[l].astype(BF16),
                           out_gain[l],
                           qk_gains, qk_scales, lam, cos_t, sin_t)
        xf, xp = _layer_norm(xf, [mix], alpha, ln1_g[l], ln1_b[l], packed=True)
        y_rows, shared, pos = _moe(xf, xp, router_w[l], router_bias[l],
                                   w_gate[l].astype(BF16), w_up[l].astype(BF16), w_down[l].astype(BF16),
                                   shared_gate[l].astype(BF16), shared_up[l].astype(BF16),
                                   shared_down[l].astype(BF16), moe_block)
        xf, xb = _combine_ln(xf, shared, y_rows, pos, alpha, ln2_g[l], ln2_b[l])
    return xf[n_s:].reshape(bp, tp, d), xf[:n_s].reshape(bs, ts, d)


def kernel(x_prompt, x_sample, ln_in_g, ln_in_b, w_in, w_out, out_gain, q_norm, k_norm, lambda_q1, lambda_k1,
           lambda_q2, lambda_k2, ln1_g, ln1_b, router_w, router_bias, w_gate, w_up, w_down, shared_gate,
           shared_up, shared_down, ln2_g, ln2_b):
    return _forward(x_prompt, x_sample, ln_in_g, ln_in_b, w_in, w_out, out_gain, q_norm, k_norm,
                    lambda_q1, lambda_k1, lambda_q2, lambda_k2, ln1_g, ln1_b, router_w, router_bias,
                    w_gate, w_up, w_down, shared_gate, shared_up, shared_down, ln2_g, ln2_b)
```

```python
import functools
import math

import jax
import jax.numpy as jnp
from jax import lax
from jax.experimental import pallas as pl
from jax.experimental.pallas import tpu as pltpu

F32 = jnp.float32
BF16 = jnp.bfloat16

HEAD_DIM = 128
GRID_W = 64
DIL_GROUPS = ((128, 1), (512, 4), (2048, 16))
A_HEADS_PER_GROUP = 4
A_HEADS = 12
B_HEADS = 8
B_QK_DIM = 64
C_HEADS = 12
C_KV_HEADS = 4
C_GROUP = C_HEADS // C_KV_HEADS
ROPE_THETA = 10000.0
A_W = A_HEADS * HEAD_DIM
B_QK_W = B_HEADS * 2 * B_QK_DIM
B_V_W = B_HEADS * HEAD_DIM
C_Q_W = C_HEADS * HEAD_DIM
C_KV_W = C_KV_HEADS * HEAD_DIM
IN_WIDTH = 3 * A_W + 2 * B_QK_W + B_V_W + C_Q_W + 2 * C_KV_W
MIX_WIDTH = (A_HEADS + B_HEADS + C_HEADS) * HEAD_DIM
OFF_QA, OFF_KA, OFF_VA = 0, A_W, 2 * A_W
OFF_QB = 3 * A_W
OFF_KB = OFF_QB + B_QK_W
OFF_VB = OFF_KB + B_QK_W
OFF_QC = OFF_VB + B_V_W
OFF_KC = OFF_QC + C_Q_W
OFF_VC = OFF_KC + C_KV_W
TOP_K = 8
N_GROUPS = 8
TOPK_GROUPS = 4
ROUTED_SCALE = 2.5
LN_EPS = 1e-5
RMS_EPS = 1e-6
NEG_BIG = -1e30

V7X_VMEM_BYTES = 64 * 1024 * 1024
VMEM_LIMIT = 52 * 1024 * 1024


def _cparams(sem):
    return pltpu.CompilerParams(dimension_semantics=sem, vmem_limit_bytes=VMEM_LIMIT)


def _tile(n, pref):
    t = min(n, pref)
    while n % t:
        t //= 2
    return t


HI_HALF_MASK = -65536


def _pack_halves(lo, hi):
    lo_bits = lax.bitcast_convert_type(lo.astype(BF16).astype(F32), jnp.int32)
    hi_bits = lax.bitcast_convert_type(hi.astype(BF16).astype(F32), jnp.int32)
    return lax.shift_right_logical(lo_bits, 16) | (hi_bits & HI_HALF_MASK)


def _unpack_lo(w):
    return lax.bitcast_convert_type(w << 16, F32)


def _unpack_hi(w):
    return lax.bitcast_convert_type(w & HI_HALF_MASK, F32)


def _layer_norm_rows(x, g_ref, b_ref):
    mu = jnp.mean(x, axis=-1, keepdims=True)
    xc = x - mu
    var = jnp.mean(xc * xc, axis=-1, keepdims=True)
    return xc * lax.rsqrt(var + LN_EPS) * g_ref[...] + b_ref[...]


def _ln_kernel(alpha, n_add, packed, *refs):
    x_ref = refs[0]
    add_refs = refs[1:1 + n_add]
    g_ref, b_ref, of_ref, o2_ref = refs[1 + n_add:]
    x = x_ref[...]
    if n_add:
        x = alpha * x
        for a in add_refs:
            x = x + a[...].astype(F32)
    y = _layer_norm_rows(x, g_ref, b_ref)
    of_ref[...] = y
    if packed:
        half = y.shape[1] // 2
        o2_ref[...] = _pack_halves(y[:, :half], y[:, half:])
    else:
        o2_ref[...] = y.astype(BF16)


def _layer_norm(x, adds, alpha, g, b, packed=False):
    n, d = x.shape
    tr = _tile(n, 256)
    row = pl.BlockSpec((tr, d), lambda i: (i, 0))
    vec = pl.BlockSpec((1, d), lambda i: (0, 0))
    second = (jax.ShapeDtypeStruct((n, d // 2), jnp.int32) if packed else jax.ShapeDtypeStruct((n, d), BF16))
    return pl.pallas_call(
        functools.partial(_ln_kernel, alpha, len(adds), packed),
        out_shape=(jax.ShapeDtypeStruct((n, d), F32), second),
        grid=(n // tr,),
        in_specs=[row] * (1 + len(adds)) + [vec, vec],
        out_specs=(row, pl.BlockSpec((tr, second.shape[1]), lambda i: (i, 0))),
        compiler_params=_cparams(("arbitrary",)),
        name="layer_norm",
    )(x, *adds, g.reshape(1, d), b.reshape(1, d))


def _mm_kernel(ksizes, *refs):
    a_refs = refs[:len(ksizes)]
    w_ref, o_ref = refs[len(ksizes):]
    acc = None
    off = 0
    for a_ref, kk in zip(a_refs, ksizes):
        part = jnp.dot(a_ref[...], w_ref[off:off + kk, :], preferred_element_type=F32)
        acc = part if acc is None else acc + part
        off += kk
    o_ref[...] = acc.astype(o_ref.dtype)


def _matmul(parts, w, layer, out_dtype):
    m = parts[0].shape[0]
    _, k, n = w.shape
    ksizes = tuple(p.shape[1] for p in parts)
    assert sum(ksizes) == k
    tm = _tile(m, 512)
    tn = _tile(n, 1024)
    in_specs = [pl.BlockSpec((tm, kk), lambda j, i: (i, 0)) for kk in ksizes]
    in_specs.append(pl.BlockSpec((None, k, tn), lambda j, i: (layer, 0, j)))
    return pl.pallas_call(
        functools.partial(_mm_kernel, ksizes),
        out_shape=jax.ShapeDtypeStruct((m, n), out_dtype),
        grid=(n // tn, m // tm),
        in_specs=in_specs,
        out_specs=pl.BlockSpec((tm, tn), lambda j, i: (i, j)),
        compiler_params=_cparams(("arbitrary", "arbitrary")),
        name="dense_projection",
    )(*parts, w)


def _head_norm(o, gain, head_scale):
    ms = jnp.mean(o * o, axis=-1, keepdims=True)
    return o * lax.rsqrt(ms + RMS_EPS) * (gain * head_scale)


def _window_attn_kernel(slopes, dil, seq_l, blk, half, q_ref, kp_ref, kc_ref, kn_ref,
                        vp_ref, vc_ref, vn_ref, o_ref, lse_ref):
    lb = pl.program_id(2)
    ii = lax.broadcasted_iota(jnp.int32, (blk, 3 * blk), 0)
    jj = lax.broadcasted_iota(jnp.int32, (blk, 3 * blk), 1)
    dist = jnp.abs(blk + ii - jj)
    kpos = (lb - 1) * blk + jj
    valid = (dist <= half) & (kpos >= 0) & (kpos < seq_l)
    tok_dist = (dist * dil).astype(F32)
    lane = lax.broadcasted_iota(jnp.int32, (blk, HEAD_DIM), 1)
    lse_tile = jnp.zeros((blk, HEAD_DIM), F32)
    scale = HEAD_DIM ** -0.5
    for h in range(A_HEADS_PER_GROUP):
        hs = slice(h * HEAD_DIM, (h + 1) * HEAD_DIM)
        q = q_ref[:, hs]
        k = jnp.concatenate([kp_ref[:, hs], kc_ref[:, hs], kn_ref[:, hs]], axis=0)
        v = jnp.concatenate([vp_ref[:, hs], vc_ref[:, hs], vn_ref[:, hs]], axis=0)
        s = lax.dot_general(q, k, (((1,), (1,)), ((), ())), preferred_element_type=F32) * scale
        s = jnp.where(valid, s - slopes[h] * tok_dist, NEG_BIG)
        m = jnp.max(s, axis=-1, keepdims=True)
        p = jnp.exp(s - m)
        den = jnp.sum(p, axis=-1, keepdims=True)
        o = jnp.dot(p.astype(BF16), v, preferred_element_type=F32)
        o_ref[:, hs] = o / den
        lse_tile = jnp.where(lane == h, m + jnp.log(den), lse_tile)
    lse_ref[...] = lse_tile


A_GROUP_W = A_HEADS_PER_GROUP * HEAD_DIM
A_PARTS = 3


LANES = 128


def _dilate_kernel(dil, x_ref, o_ref, scr):
    rows = scr.shape[1] // dil
    for c in range(scr.shape[0]):
        scr[c] = x_ref[:, c * LANES:(c + 1) * LANES].astype(F32)
        for r in range(dil):
            c0 = r * A_GROUP_W + c * LANES
            o_ref[:, c0:c0 + LANES] = scr[c, pl.ds(r, rows, stride=dil), :].astype(BF16)


def _dilate(proj, group):
    _, dil = DIL_GROUPS[group]
    n = proj.shape[0]
    tm = _tile(n, 256)
    return pl.pallas_call(
        functools.partial(_dilate_kernel, dil),
        out_shape=jax.ShapeDtypeStruct((n // dil, A_PARTS * dil * A_GROUP_W), BF16),
        grid=(n // tm, A_PARTS),
        in_specs=[pl.BlockSpec((tm, A_GROUP_W), lambda i, p: (i, p * (A_W // A_GROUP_W) + group))],
        out_specs=pl.BlockSpec((tm // dil, dil * A_GROUP_W), lambda i, p: (i, p)),
        scratch_shapes=[pltpu.VMEM((A_GROUP_W // LANES, tm, LANES), F32)],
        compiler_params=_cparams(("arbitrary", "arbitrary")),
        name="dilate_rows",
    )(proj)


def _window_attention(src, row0, nseq, seq_t, group):
    window, dil = DIL_GROUPS[group]
    half = (window // 2) // dil
    seq_l = seq_t // dil
    blk = _tile(seq_l, 128)
    nlb = seq_l // blk
    base = row0 // dil // blk
    cw = A_GROUP_W
    slopes = tuple(2.0 ** (-8.0 * (group * A_HEADS_PER_GROUP + h + 1) / A_HEADS)
                   for h in range(A_HEADS_PER_GROUP))

    def spec(part, shift):
        def imap(b, r, lb):
            nb = jnp.clip(lb + shift, 0, nlb - 1)
            col = part * (A_W // cw) + group if dil == 1 else part * dil + r
            return (base + b * nlb + nb, col)
        return pl.BlockSpec((blk, cw), imap)

    rows = nseq * seq_l
    return pl.pallas_call(
        functools.partial(_window_attn_kernel, slopes, dil, seq_l, blk, half),
        out_shape=(jax.ShapeDtypeStruct((rows, dil * cw), F32),
                   jax.ShapeDtypeStruct((rows, dil * HEAD_DIM), F32)),
        grid=(nseq, dil, nlb),
        in_specs=[spec(0, 0), spec(1, -1), spec(1, 0), spec(1, 1), spec(2, -1), spec(2, 0), spec(2, 1)],
        out_specs=(pl.BlockSpec((blk, cw), lambda b, r, lb: (b * nlb + lb, r)),
                   pl.BlockSpec((blk, HEAD_DIM), lambda b, r, lb: (b * nlb + lb, r))),
        compiler_params=_cparams(("arbitrary", "arbitrary", "arbitrary")),
        name="window_attention",
    )(src, src, src, src, src, src, src)


def _undilate(ref, dil, scr):
    if dil == 1:
        return ref[...]
    rows, w = ref.shape[0], ref.shape[1] // dil
    for c in range(w // LANES):
        for r in range(dil):
            c0 = r * w + c * LANES
            scr[c, pl.ds(r, rows, stride=dil), :] = ref[:, c0:c0 + LANES]
    return jnp.concatenate([scr[c] for c in range(w // LANES)], axis=1)


def _window_merge_kernel(o0_ref, o1_ref, o2_ref, l0_ref, l1_ref, l2_ref, gain_ref, out_ref,
                         so1, so2, sl1, sl2):
    dils = tuple(d for _, d in DIL_GROUPS)
    outs = tuple(_undilate(r, d, s) for r, d, s in zip((o0_ref, o1_ref, o2_ref), dils, (None, so1, so2)))
    lses = tuple(_undilate(r, d, s) for r, d, s in zip((l0_ref, l1_ref, l2_ref), dils, (None, sl1, sl2)))
    mx = jnp.maximum(jnp.maximum(lses[0], lses[1]), lses[2])
    es = [jnp.exp(l - mx) for l in lses]
    tot = es[0] + es[1] + es[2]
    for g in range(len(DIL_GROUPS)):
        alpha = es[g] / tot
        for h in range(A_HEADS_PER_GROUP):
            hs = slice(h * HEAD_DIM, (h + 1) * HEAD_DIM)
            c0 = (g * A_HEADS_PER_GROUP + h) * HEAD_DIM
            o = outs[g][:, hs] * alpha[:, h:h + 1]
            out_ref[:, c0:c0 + HEAD_DIM] = _head_norm(o, gain_ref[:, c0:c0 + HEAD_DIM], 1.0).astype(BF16)


def _window_merge(outs, lses, gain_a):
    n = outs[0].shape[0]
    tr = _tile(n, 256)
    dils = tuple(d for _, d in DIL_GROUPS)
    return pl.pallas_call(
        _window_merge_kernel,
        out_shape=jax.ShapeDtypeStruct((n, A_W), BF16),
        grid=(n // tr,),
        in_specs=[pl.BlockSpec((tr // d, d * A_GROUP_W), lambda i: (i, 0)) for d in dils]
        + [pl.BlockSpec((tr // d, d * HEAD_DIM), lambda i: (i, 0)) for d in dils]
        + [pl.BlockSpec((1, A_W), lambda i: (0, 0))],
        out_specs=pl.BlockSpec((tr, A_W), lambda i: (i, 0)),
        scratch_shapes=[pltpu.VMEM((A_GROUP_W // LANES, tr, LANES), F32)] * 2
        + [pltpu.VMEM((HEAD_DIM // LANES, tr, LANES), F32)] * 2,
        compiler_params=_cparams(("arbitrary",)),
        name="window_merge",
    )(*outs, *lses, gain_a)


LOG2E = math.log2(math.e)
ATTN_KV_CHUNK = 1024
DIFF_Q_BLOCK = 512
GQA_Q_BLOCK = 256


def _flash_loop(q, k_ref, v_ref, sa_ref, sb_ref, m_ref, acc_ref, tkc, parts, bias_fn):
    n_chunks = k_ref.shape[0] // tkc
    ones = jnp.ones((tkc, HEAD_DIM), BF16)
    m_ref[...] = jnp.full(m_ref.shape, -jnp.inf, F32)
    acc_ref[...] = jnp.zeros(acc_ref.shape, F32)

    def scores(c):
        ks = pl.ds(pl.multiple_of(c * tkc, tkc), tkc)
        return lax.dot_general(q, k_ref[ks, :], (((1,), (1,)), ((), ())), preferred_element_type=F32)

    def consume(s_ref, c):
        ks = pl.ds(pl.multiple_of(c * tkc, tkc), tkc)
        v_ext = jnp.concatenate([v_ref[ks, :], ones], axis=1)
        bias = bias_fn(c)
        for r0, r1 in parts:
            u = s_ref[r0:r1, :] if bias is None else s_ref[r0:r1, :] + bias
            m_prev = m_ref[r0:r1, :]
            m_new = jnp.maximum(m_prev, jnp.max(u, axis=-1, keepdims=True))
            p = jnp.exp2((u - m_new).astype(BF16))
            alpha = jnp.exp2(m_prev - m_new)
            acc_ref[r0:r1, :] = alpha * acc_ref[r0:r1, :] + jnp.dot(p, v_ext, preferred_element_type=F32)
            m_ref[r0:r1, :] = m_new

    sa_ref[...] = scores(0)

    def body(i, carry):
        c = 2 * i
        sb_ref[...] = scores(c + 1)
        consume(sa_ref, c)
        sa_ref[...] = scores(jnp.minimum(c + 2, n_chunks - 1))
        consume(sb_ref, c + 1)
        return carry

    lax.fori_loop(0, n_chunks // 2, body, 0)


def _softmax_out(acc_ref, rows):
    acc = acc_ref[rows, :]
    return acc[:, :HEAD_DIM] / acc[:, HEAD_DIM:]


def _diff_attn_kernel(tq, tkc, head_scale, slope_ref, lam_ref, gain_ref, q_ref, k_ref, v_ref, o_ref,
                      sa_ref, sb_ref, m_ref, acc_ref):
    q0 = pl.program_id(2) * tq
    q = q_ref[...]
    lane = lax.broadcasted_iota(jnp.int32, q.shape, 1)
    zero = jnp.zeros_like(q)
    q = jnp.concatenate([jnp.where(lane < B_QK_DIM, q, zero), jnp.where(lane >= B_QK_DIM, q, zero)], axis=0)
    slope2 = slope_ref[pl.program_id(1)]
    dij = (lax.broadcasted_iota(jnp.int32, (tq, tkc), 0)
           - lax.broadcasted_iota(jnp.int32, (tq, tkc), 1)).astype(F32)

    def bias_fn(c):
        d = (q0 - c * tkc).astype(F32)
        return -slope2 * jnp.abs(dij + d)

    _flash_loop(q, k_ref, v_ref, sa_ref, sb_ref, m_ref, acc_ref, tkc, ((0, tq), (tq, 2 * tq)), bias_fn)
    o = _softmax_out(acc_ref, slice(0, tq)) - lam_ref[0] * _softmax_out(acc_ref, slice(tq, 2 * tq))
    o_ref[...] = _head_norm(o, gain_ref[...], head_scale).astype(BF16)


def _kv_chunk(seq_t):
    return _tile(seq_t // 2, ATTN_KV_CHUNK)


def _diff_attention(proj, row0, nseq, seq_t, lam, lam_init, gain_b):
    tq = _tile(seq_t, DIFF_Q_BLOCK)
    tkc = _kv_chunk(seq_t)
    nq = seq_t // tq
    slopes2 = jnp.asarray([2.0 ** (-8.0 * (h + 1) / B_HEADS) * LOG2E for h in range(B_HEADS)], F32)
    smem = pl.BlockSpec(memory_space=pltpu.SMEM)
    qcol, kcol, vcol = OFF_QB // HEAD_DIM, OFF_KB // HEAD_DIM, OFF_VB // HEAD_DIM
    return pl.pallas_call(
        functools.partial(_diff_attn_kernel, tq, tkc, 1.0 - lam_init),
        out_shape=jax.ShapeDtypeStruct((nseq * seq_t, B_V_W), BF16),
        grid=(nseq, B_HEADS, nq),
        in_specs=[smem, smem,
                  pl.BlockSpec((1, HEAD_DIM), lambda b, h, qi: (0, h)),
                  pl.BlockSpec((tq, HEAD_DIM), lambda b, h, qi: (row0 // tq + b * nq + qi, qcol + h)),
                  pl.BlockSpec((seq_t, HEAD_DIM), lambda b, h, qi: (row0 // seq_t + b, kcol + h)),
                  pl.BlockSpec((seq_t, HEAD_DIM), lambda b, h, qi: (row0 // seq_t + b, vcol + h))],
        out_specs=pl.BlockSpec((tq, HEAD_DIM), lambda b, h, qi: (b * nq + qi, h)),
        scratch_shapes=[pltpu.VMEM((2 * tq, tkc), F32), pltpu.VMEM((2 * tq, tkc), F32),
                        pltpu.VMEM((2 * tq, 1), F32), pltpu.VMEM((2 * tq, 2 * HEAD_DIM), F32)],
        compiler_params=_cparams(("arbitrary", "arbitrary", "arbitrary")),
        name="diff_attention",
    )(slopes2, lam.reshape(1), gain_b, proj, proj, proj)


def _qk_prep_kernel(proj_ref, gain_ref, scale_ref, cos_ref, sin_ref, o_ref):
    j = pl.program_id(1)
    x = proj_ref[...].astype(F32)
    ms = jnp.mean(x * x, axis=-1, keepdims=True)
    y = x * lax.rsqrt(ms + RMS_EPS) * gain_ref[pl.ds(j, 1), :]
    lane = lax.broadcasted_iota(jnp.int32, y.shape, 1)
    partner = jnp.where((lane & 1) == 0, pltpu.roll(y, HEAD_DIM - 1, 1), pltpu.roll(y, 1, 1))
    o_ref[...] = ((y * cos_ref[...] + partner * sin_ref[...]) * scale_ref[pl.ds(j, 1), :]).astype(BF16)


def _qk_prep(proj, gains, scales, cos_t, sin_t):
    n = proj.shape[0]
    tr = _tile(n, 512)
    nh = C_HEADS + C_KV_HEADS
    c0 = OFF_QC // HEAD_DIM
    full = pl.BlockSpec((nh, HEAD_DIM), lambda i, j: (0, 0))
    tab = pl.BlockSpec((tr, HEAD_DIM), lambda i, j: (i, 0))
    return pl.pallas_call(
        _qk_prep_kernel,
        out_shape=jax.ShapeDtypeStruct((n, nh * HEAD_DIM), BF16),
        grid=(n // tr, nh),
        in_specs=[pl.BlockSpec((tr, HEAD_DIM), lambda i, j: (i, c0 + j)), full, full, tab, tab],
        out_specs=pl.BlockSpec((tr, HEAD_DIM), lambda i, j: (i, j)),
        compiler_params=_cparams(("arbitrary", "arbitrary")),
        name="qk_norm_rope",
    )(proj, gains, scales, cos_t, sin_t)


def _gqa_kernel(tq, tkc, gain_ref, q_ref, k_ref, v_ref, o_ref, sa_ref, sb_ref, m_ref, acc_ref):
    q = jnp.concatenate([q_ref[:, g * HEAD_DIM:(g + 1) * HEAD_DIM] for g in range(C_GROUP)], axis=0)
    _flash_loop(q, k_ref, v_ref, sa_ref, sb_ref, m_ref, acc_ref, tkc, ((0, C_GROUP * tq),), lambda c: None)
    for g in range(C_GROUP):
        gs = slice(g * HEAD_DIM, (g + 1) * HEAD_DIM)
        o = _softmax_out(acc_ref, slice(g * tq, (g + 1) * tq))
        o_ref[:, gs] = _head_norm(o, gain_ref[:, gs], 1.0).astype(BF16)


def _gqa_attention(qk_rot, proj, row0, nseq, seq_t, gain_c):
    tq = _tile(seq_t, GQA_Q_BLOCK)
    tkc = _kv_chunk(seq_t)
    nq = seq_t // tq
    gw = C_GROUP * HEAD_DIM
    vcol = OFF_VC // HEAD_DIM
    rows = C_GROUP * tq
    return pl.pallas_call(
        functools.partial(_gqa_kernel, tq, tkc),
        out_shape=jax.ShapeDtypeStruct((nseq * seq_t, C_Q_W), BF16),
        grid=(nseq, C_KV_HEADS, nq),
        in_specs=[pl.BlockSpec((1, gw), lambda b, h, qi: (0, h)),
                  pl.BlockSpec((tq, gw), lambda b, h, qi: (row0 // tq + b * nq + qi, h)),
                  pl.BlockSpec((seq_t, HEAD_DIM), lambda b, h, qi: (row0 // seq_t + b, C_HEADS + h)),
                  pl.BlockSpec((seq_t, HEAD_DIM), lambda b, h, qi: (row0 // seq_t + b, vcol + h))],
        out_specs=pl.BlockSpec((tq, gw), lambda b, h, qi: (b * nq + qi, h)),
        scratch_shapes=[pltpu.VMEM((rows, tkc), F32), pltpu.VMEM((rows, tkc), F32),
                        pltpu.VMEM((rows, 1), F32), pltpu.VMEM((rows, 2 * HEAD_DIM), F32)],
        compiler_params=_cparams(("arbitrary", "arbitrary", "arbitrary")),
        name="gqa_attention",
    )(gain_c, qk_rot, qk_rot, proj)


def _rope_tables(seq_t):
    rows = seq_t // GRID_W
    r = jnp.repeat(jnp.arange(rows, dtype=F32), GRID_W)
    c = jnp.tile(jnp.arange(GRID_W, dtype=F32), rows)
    half = HEAD_DIM // 2
    inv_freq = ROPE_THETA ** (-jnp.arange(0, half, 2, dtype=F32) / half)
    ang = jnp.concatenate([r[:, None] * inv_freq, c[:, None] * inv_freq], -1)
    cos = jnp.repeat(jnp.cos(ang), 2, axis=-1)
    sin = jnp.repeat(jnp.sin(ang), 2, axis=-1) * jnp.tile(jnp.asarray([-1.0, 1.0], F32), half)
    return cos, sin


def _first_argmax(v, lane, n):
    m = jnp.max(v, axis=-1, keepdims=True)
    idx = jnp.min(jnp.where(v == m, lane, float(n)), axis=-1, keepdims=True)
    return m, idx


def _router_kernel(n_exp, x_ref, wh_ref, wl_ref, bias_ref, idx_ref, wt_ref):
    x = x_ref[...]
    xh = x.astype(BF16)
    xl = (x - xh.astype(F32)).astype(BF16)
    logits = (jnp.dot(xh, wh_ref[...], preferred_element_type=F32)
              + jnp.dot(xh, wl_ref[...], preferred_element_type=F32)
              + jnp.dot(xl, wh_ref[...], preferred_element_type=F32))
    scores = jax.nn.sigmoid(logits)
    biased = scores + bias_ref[...]
    tr = x.shape[0]
    gsz = n_exp // N_GROUPS
    lane_i = lax.broadcasted_iota(jnp.int32, (tr, n_exp), 1)
    lane = lane_i.astype(F32)
    grp = (lane_i // gsz).astype(F32)
    neg = -jnp.inf
    gscore = jnp.zeros((tr, n_exp), F32)
    for g in range(N_GROUPS):
        in_g = grp == float(g)
        vg = jnp.where(in_g, biased, neg)
        m1, i1 = _first_argmax(vg, lane, n_exp)
        m2 = jnp.max(jnp.where(lane == i1, neg, vg), axis=-1, keepdims=True)
        gscore = jnp.where(in_g, m1 + m2, gscore)
    cand = jnp.full((tr, n_exp), neg, F32)
    for _ in range(TOPK_GROUPS):
        _, i = _first_argmax(gscore, lane, n_exp)
        pick = grp == jnp.floor(i * (1.0 / gsz))
        cand = jnp.where(pick, biased, cand)
        gscore = jnp.where(pick, neg, gscore)
    lane_o = lax.broadcasted_iota(jnp.int32, idx_ref.shape, 1)
    idx_out = jnp.zeros(idx_ref.shape, F32)
    wt_out = jnp.zeros(wt_ref.shape, F32)
    for kk in range(TOP_K):
        _, i = _first_argmax(cand, lane, n_exp)
        pick = lane == i
        sk = jnp.sum(jnp.where(pick, scores, 0.0), axis=-1, keepdims=True)
        cand = jnp.where(pick, neg, cand)
        idx_out = jnp.where(lane_o == kk, i, idx_out)
        wt_out = jnp.where(lane_o == kk, sk, wt_out)
    idx_ref[...] = idx_out.astype(jnp.int32)
    wt_ref[...] = wt_out /jnp.sum(wt_out, axis=-1, keepdims=True) * ROUTED_SCALE


def _router(x, router_w, router_bias):
    n, d = x.shape
    n_exp = router_w.shape[1]
    tr = _tile(n, 256)
    wh = router_w.astype(BF16)
    wl = (router_w - wh.astype(F32)).astype(BF16)
    full = pl.BlockSpec((d, n_exp), lambda i: (0, 0))
    out = pl.BlockSpec((tr, HEAD_DIM), lambda i: (i, 0))
    idx, wt = pl.pallas_call(
        functools.partial(_router_kernel, n_exp),
        out_shape=(jax.ShapeDtypeStruct((n, HEAD_DIM), jnp.int32), jax.ShapeDtypeStruct((n, HEAD_DIM), F32)),
        grid=(n // tr,),
        in_specs=[pl.BlockSpec((tr, d), lambda i: (i, 0)), full, full,
                  pl.BlockSpec((1, n_exp), lambda i: (0, 0))],
        out_specs=(out, out),
        compiler_params=_cparams(("arbitrary",)),
        name="moe_router",
    )(x, wh, wl, router_bias.reshape(1, n_exp).astype(F32))
    return idx[:, :TOP_K], wt[:, :TOP_K]


ROW_DMA_UNROLL = 8


def _row_dma_burst(n_rows, copy):
    assert n_rows % ROW_DMA_UNROLL == 0

    def start(i, c):
        for u in range(ROW_DMA_UNROLL):
            copy(i * ROW_DMA_UNROLL + u).start(priority=u % 2)
        return c

    def wait(i, c):
        for u in range(ROW_DMA_UNROLL):
            copy(i * ROW_DMA_UNROLL + u).wait()
        return c

    lax.fori_loop(0, n_rows // ROW_DMA_UNROLL, start, 0)
    lax.fori_loop(0, n_rows // ROW_DMA_UNROLL, wait, 0)


def _gather_rows_kernel(bm, nused_ref, tok_ref, x_hbm, o_ref, sem):
    b = pl.program_id(0)

    @pl.when(b < nused_ref[0])
    def _():
        def copy(r):
            return pltpu.make_async_copy(x_hbm.at[pl.ds(tok_ref[0, 0, r], 1), :],
                                         o_ref.at[pl.ds(r, 1), :], sem)

        _row_dma_burst(bm, copy)

    @pl.when(b >= nused_ref[0])
    def _():
        o_ref[...] = jnp.zeros(o_ref.shape, o_ref.dtype)


def _gather_rows(x, row_tok, n_used, bm):
    n, d = x.shape
    n_blocks = row_tok.shape[0] // bm

    def oidx(b, nused):
        return (b, 0)

    return pl.pallas_call(
        functools.partial(_gather_rows_kernel, bm),
        out_shape=jax.ShapeDtypeStruct((n_blocks * bm, d), x.dtype),
        grid_spec=pltpu.PrefetchScalarGridSpec(
            num_scalar_prefetch=1,
            grid=(n_blocks,),
            in_specs=[pl.BlockSpec((1, 1, bm), lambda b, nused: (b, 0, 0), memory_space=pltpu.SMEM),
                      pl.BlockSpec(memory_space=pl.ANY)],
            out_specs=pl.BlockSpec((bm, d), oidx),
            scratch_shapes=[pltpu.SemaphoreType.DMA],
        ),
        compiler_params=_cparams(("arbitrary",)),
        name="moe_dispatch_gather",
    )(n_used, row_tok.reshape(n_blocks, 1, bm), x)


def _gate_up_kernel(be_ref, nused_ref, x_ref, wg_ref, wu_ref, h_ref, g_acc, u_acc):
    b = pl.program_id(0)
    kc = pl.program_id(1)

    @pl.when(b < nused_ref[0])
    def _():
        w = x_ref[...]
        x = lax.bitcast_convert_type(jnp.where(kc == 0, w << 16, w & HI_HALF_MASK), F32).astype(BF16)
        g = jnp.dot(x, wg_ref[0], preferred_element_type=F32)
        u = jnp.dot(x, wu_ref[0], preferred_element_type=F32)

        @pl.when(kc == 0)
        def _():
            g_acc[...] = g
            u_acc[...] = u

        @pl.when(kc == 1)
        def _():
            gg = g_acc[...] + g
            h_ref[...] = (gg * jax.nn.sigmoid(gg) * (u_acc[...] + u)).astype(BF16)

    @pl.when((b >= nused_ref[0]) & (kc == 1))
    def _():
        h_ref[...] = jnp.zeros(h_ref.shape, h_ref.dtype)


def _down_kernel(be_ref, nused_ref, h_ref, wlo_ref, whi_ref, rw_ref, o_ref):
    b = pl.program_id(0)

    @pl.when(b < nused_ref[0])
    def _():
        h = h_ref[...]
        rw = rw_ref[...]
        o_ref[...] = _pack_halves(jnp.dot(h, wlo_ref[0], preferred_element_type=F32) * rw,
                                  jnp.dot(h, whi_ref[0], preferred_element_type=F32) * rw)

    @pl.when(b >= nused_ref[0])
    def _():
        o_ref[...] = jnp.zeros(o_ref.shape, o_ref.dtype)


def _expert_ffn(x_rows, w_gate, w_up, w_down, row_w, block_e, n_used, bm):
    rows, dh = x_rows.shape
    ff = w_gate.shape[-1]
    n_blocks = rows // bm

    def live(b, nused):
        return jnp.minimum(b, nused[0] - 1)

    def kclamp(b, kc, nused):
        return jnp.where(b < nused[0], kc, 1)

    hmid = pl.pallas_call(
        _gate_up_kernel,
        out_shape=jax.ShapeDtypeStruct((rows, ff), BF16),
        grid_spec=pltpu.PrefetchScalarGridSpec(
            num_scalar_prefetch=2,
            grid=(n_blocks, 2),
            in_specs=[pl.BlockSpec((bm, dh), lambda b, kc, be, nused: (live(b, nused), 0)),
                      pl.BlockSpec((1, dh, ff), lambda b, kc, be, nused: (be[b], kclamp(b, kc, nused), 0)),
                      pl.BlockSpec((1, dh, ff), lambda b, kc, be, nused: (be[b], kclamp(b, kc, nused), 0))],
            out_specs=pl.BlockSpec((bm, ff), lambda b, kc, be, nused: (b, 0)),
            scratch_shapes=[pltpu.VMEM((bm, ff), F32), pltpu.VMEM((bm, ff), F32)],
        ),
        compiler_params=_cparams(("arbitrary", "arbitrary")),
        name="expert_gate_up",
    )(block_e, n_used, x_rows, w_gate, w_up)

    tn = _tile(dh, 1024)
    nn = dh // tn
    return pl.pallas_call(
        _down_kernel,
        out_shape=jax.ShapeDtypeStruct((rows, dh), jnp.int32),
        grid_spec=pltpu.PrefetchScalarGridSpec(
            num_scalar_prefetch=2,
            grid=(n_blocks, nn),
            in_specs=[pl.BlockSpec((bm, ff), lambda b, j, be, nused: (live(b, nused), 0)),
                      pl.BlockSpec((1, ff, tn), lambda b, j, be, nused: (be[b], 0, j)),
                      pl.BlockSpec((1, ff, tn), lambda b, j, be, nused: (be[b], 0, nn + j)),
                      pl.BlockSpec((bm, 1), lambda b, j, be, nused: (live(b, nused), 0))],
            out_specs=pl.BlockSpec((bm, tn), lambda b, j, be, nused: (b, j)),
        ),
        compiler_params=_cparams(("arbitrary", "arbitrary")),
        name="expert_down",
    )(block_e, n_used, hmid, w_down, w_down, row_w.reshape(rows, 1))


def _combine_ln_kernel(tr, alpha, pos_ref, x_ref, sh_ref, g_ref, b_ref, y_hbm, of_ref, ob_ref, buf, sem):
    def copy(i):
        return pltpu.make_async_copy(y_hbm.at[pl.ds(pos_ref[0, 0, i], 1), :], buf.at[pl.ds(i, 1), :], sem)

    _row_dma_burst(tr * TOP_K, copy)
    w = sh_ref[...]
    lo, hi = _unpack_lo(w), _unpack_hi(w)
    for kk in range(TOP_K):
        w = buf[kk * tr:(kk + 1) * tr, :]
        lo, hi = lo + _unpack_lo(w), hi + _unpack_hi(w)
    x = alpha * x_ref[...] + jnp.concatenate([lo, hi], axis=1)
    y = _layer_norm_rows(x, g_ref, b_ref)
    of_ref[...] = y
    ob_ref[...] = y.astype(BF16)


def _combine_ln(x, shared, y_rows, pos, alpha, g, b):
    n, d = x.shape
    tr = _tile(n, 64)
    nt = n // tr
    pos_t = pos.reshape(nt, tr, TOP_K).transpose(0, 2, 1).reshape(nt, 1, TOP_K * tr)
    row = pl.BlockSpec((tr, d), lambda i: (i, 0))
    half_row = pl.BlockSpec((tr, d // 2), lambda i: (i, 0))
    vec = pl.BlockSpec((1, d), lambda i: (0, 0))
    return pl.pallas_call(
        functools.partial(_combine_ln_kernel, tr, alpha),
        out_shape=(jax.ShapeDtypeStruct((n, d), F32), jax.ShapeDtypeStruct((n, d), BF16)),
        grid=(nt,),
        in_specs=[pl.BlockSpec((1, 1, TOP_K * tr), lambda i: (i, 0, 0), memory_space=pltpu.SMEM),
                  row, half_row, vec, vec, pl.BlockSpec(memory_space=pl.ANY)],
        out_specs=(row, row),
        scratch_shapes=[pltpu.VMEM((TOP_K * tr, d // 2), jnp.int32), pltpu.SemaphoreType.DMA],
        compiler_params=_cparams(("arbitrary",)),
        name="moe_combine_layer_norm",
    )(pos_t, x, shared, g.reshape(1, d), b.reshape(1, d), y_rows)


def _dispatch_plan(idx, wts, n_exp, bm):
    n = idx.shape[0]
    nk = n * TOP_K
    n_blocks = -(-nk // bm) + n_exp
    rows = n_blocks * bm
    flat_e = idx.reshape(-1)
    experts = jnp.arange(n_exp, dtype=jnp.int32)
    order = jnp.argsort(flat_e, stable=True).astype(jnp.int32)
    inv = jnp.argsort(order).astype(jnp.int32)
    counts = jnp.sum((flat_e[:, None] == experts[None, :]).astype(jnp.int32), axis=0)
    padded = (counts + bm - 1) // bm * bm
    starts = jnp.cumsum(counts) - counts
    pends = jnp.cumsum(padded)
    pstarts = pends - padded
    blk_start = jnp.arange(n_blocks, dtype=jnp.int32) * bm
    block_e = jnp.minimum(jnp.sum((pends[None, :] <= blk_start[:, None]).astype(jnp.int32), axis=1), n_exp - 1)
    n_used = (pends[-1] // bm).astype(jnp.int32).reshape(1)
    row = jnp.arange(rows, dtype=jnp.int32)
    row_e = jnp.repeat(block_e, bm)
    off = row - pstarts[row_e]
    valid = off < counts[row_e]
    pair = order[jnp.clip(starts[row_e] + off, 0, nk - 1)]
    row_tok = jnp.where(valid, pair // TOP_K, 0)
    row_w = jnp.where(valid, wts.reshape(-1)[pair], 0.0)
    pos = (pstarts[flat_e] + inv - starts[flat_e]).reshape(n, TOP_K)
    return row_tok, row_w, pos, block_e, n_used


def _token_mixer(xb, seqs, layer, lam_init, w_in, w_out, out_gain_l, qk_gains, qk_scales, lam, cos_t, sin_t):
    proj = _matmul([xb], w_in, layer, BF16)
    qk_rot = _qk_prep(proj, qk_gains, qk_scales, cos_t, sin_t)
    window_src = [proj if dil == 1 else _dilate(proj, g) for g, (_, dil) in enumerate(DIL_GROUPS)]
    gain = out_gain_l.reshape(1, MIX_WIDTH).astype(F32)
    gain_a, gain_b, gain_c = gain[:, :A_W], gain[:, A_W:A_W + B_V_W], gain[:, A_W + B_V_W:]
    a_parts, b_parts, c_parts = [], [], []
    for row0, nseq, seq_t in seqs:
        outs, lses = zip(*[_window_attention(window_src[g], row0, nseq, seq_t, g)
                           for g in range(len(DIL_GROUPS))])
        a_parts.append(_window_merge(outs, lses, gain_a))
        b_parts.append(_diff_attention(proj, row0, nseq, seq_t, lam, lam_init, gain_b))
        c_parts.append(_gqa_attention(qk_rot, proj, row0, nseq, seq_t, gain_c))
    parts = [jnp.concatenate(p, axis=0) for p in (a_parts, b_parts, c_parts)]
    return _matmul(parts, w_out, layer, F32)


def _moe(xf, xp, layer, router_w, router_bias, w_gate, w_up, w_down, sh_gate, sh_up, sh_down, bm):
    n = xf.shape[0]
    n_exp = router_w.shape[1]
    idx, wts = _router(xf, router_w, router_bias)
    row_tok, row_w, pos, block_e, n_used = _dispatch_plan(idx, wts, n_exp, bm)
    x_rows = _gather_rows(xp, row_tok, n_used, bm)
    y_rows = _expert_ffn(x_rows, w_gate, w_up, w_down, row_w, block_e + layer * n_exp, n_used, bm)
    sb = _tile(n, bm)
    shared = _expert_ffn(xp, sh_gate, sh_up, sh_down, jnp.ones((n,), F32),
                         jnp.full((n // sb,), layer, jnp.int32), jnp.full((1,), n // sb, jnp.int32), sb)
    return y_rows, shared, pos


def _forward(x_prompt, x_sample, ln_in_g, ln_in_b, w_in, w_out, out_gain, q_norm, k_norm,
             lambda_q1, lambda_k1, lambda_q2, lambda_k2, ln1_g, ln1_b, router_w, router_bias,
             w_gate, w_up, w_down, shared_gate, shared_up, shared_down, ln2_g, ln2_b, moe_block=512):
    depth = w_in.shape[0]
    d = x_prompt.shape[-1]
    alpha = (2 * depth) ** 0.25
    bp, tp, _ = x_prompt.shape
    bs, ts, _ = x_sample.shape
    n_s = bs * ts
    assert ts >= tp and n_s % tp == 0
    seqs = ((0, bs, ts), (n_s, bp, tp))
    x = jnp.concatenate([x_sample.reshape(n_s, d), x_prompt.reshape(bp * tp, d)], axis=0)

    tabs = [_rope_tables(t) for t in (ts, tp)]
    cos_t = jnp.concatenate([jnp.tile(tabs[0][0], (bs, 1)), jnp.tile(tabs[1][0], (bp, 1))], axis=0)
    sin_t = jnp.concatenate([jnp.tile(tabs[0][1], (bs, 1)), jnp.tile(tabs[1][1], (bp, 1))], axis=0)
    qk_scales = jnp.concatenate([jnp.full((C_HEADS, HEAD_DIM), HEAD_DIM ** -0.5 * LOG2E, F32),
                                 jnp.ones((C_KV_HEADS, HEAD_DIM), F32)], axis=0)
    in_col_scale = jnp.ones((IN_WIDTH,), F32).at[OFF_QB:OFF_KB].set(B_QK_DIM ** -0.5 * LOG2E)

    w_in_b = (w_in * in_col_scale).astype(BF16)
    w_out_b = w_out.astype(BF16)
    expert_w = [w.astype(BF16).reshape((depth * w.shape[1],) + w.shape[2:]) for w in (w_gate, w_up, w_down)]
    shared_w = [w.astype(BF16) for w in (shared_gate, shared_up, shared_down)]

    xf, xb = _layer_norm(x, [], 1.0, ln_in_g, ln_in_b)
    for l in range(depth):
        lam_init = 0.8 - 0.6 * math.exp(-0.3 * l)
        lam = (jnp.exp(jnp.sum(lambda_q1[l].astype(F32) * lambda_k1[l].astype(F32)))
               - jnp.exp(jnp.sum(lambda_q2[l].astype(F32) * lambda_k2[l].astype(F32))) + lam_init)
        qk_gains = jnp.concatenate([jnp.tile(q_norm[l][None].astype(F32), (C_HEADS, 1)),
                                    jnp.tile(k_norm[l][None].astype(F32), (C_KV_HEADS, 1))], axis=0)
        mix = _token_mixer(xb, seqs, l, lam_init, w_in_b, w_out_b, out_gain[l],
                           qk_gains, qk_scales, lam, cos_t, sin_t)
        xf, xp = _layer_norm(xf, [mix], alpha, ln1_g[l], ln1_b[l], packed=True)
        y_rows, shared, pos = _moe(xf, xp, l, router_w[l], router_bias[l], *expert_w, *shared_w, moe_block)
        xf, xb = _combine_ln(xf, shared, y_rows, pos, alpha, ln2_g[l], ln2_b[l])
    return xf[n_s:].reshape(bp, tp, d), xf[:n_s].reshape(bs, ts, d)


def kernel(x_prompt, x_sample, ln_in_g, ln_in_b, w_in, w_out, out_gain, q_norm, k_norm, lambda_q1, lambda_k1,
           lambda_q2, lambda_k2, ln1_g, ln1_b, router_w, router_bias, w_gate, w_up, w_down, shared_gate,
           shared_up, shared_down, ln2_g, ln2_b):
    return _forward(x_prompt, x_sample, ln_in_g, ln_in_b, w_in, w_out, out_gain, q_norm, k_norm,
                    lambda_q1, lambda_k1, lambda_q2, lambda_k2, ln1_g, ln1_b, router_w, router_bias,
                    w_gate, w_up, w_down, shared_gate, shared_up, shared_down, ln2_g, ln2_b)
```

```python
import functools
import math

import jax
import jax.numpy as jnp
from jax import lax
from jax.experimental import pallas as pl
from jax.experimental.pallas import tpu as pltpu

F32 = jnp.float32
BF16 = jnp.bfloat16

HEAD_DIM = 128
GRID_W = 64
DIL_GROUPS = ((128, 1), (512, 4), (2048, 16))
A_HEADS_PER_GROUP = 4
A_HEADS = 12
B_HEADS = 8
B_QK_DIM = 64
C_HEADS = 12
C_KV_HEADS = 4
C_GROUP = C_HEADS // C_KV_HEADS
ROPE_THETA = 10000.0
A_W = A_HEADS * HEAD_DIM
B_QK_W = B_HEADS * 2 * B_QK_DIM
B_V_W = B_HEADS * HEAD_DIM
C_Q_W = C_HEADS * HEAD_DIM
C_KV_W = C_KV_HEADS * HEAD_DIM
IN_WIDTH = 3 * A_W + 2 * B_QK_W + B_V_W + C_Q_W + 2 * C_KV_W
MIX_WIDTH = (A_HEADS + B_HEADS + C_HEADS) * HEAD_DIM
OFF_QA, OFF_KA, OFF_VA = 0, A_W, 2 * A_W
OFF_QB = 3 * A_W
OFF_KB = OFF_QB + B_QK_W
OFF_VB = OFF_KB + B_QK_W
OFF_QC = OFF_VB + B_V_W
OFF_KC = OFF_QC + C_Q_W
OFF_VC = OFF_KC + C_KV_W
TOP_K = 8
N_GROUPS = 8
TOPK_GROUPS = 4
ROUTED_SCALE = 2.5
LN_EPS = 1e-5
RMS_EPS = 1e-6
NEG_BIG = -1e30

V7X_VMEM_BYTES = 64 * 1024 * 1024
VMEM_LIMIT = 52 * 1024 * 1024


def _cparams(sem):
    return pltpu.CompilerParams(dimension_semantics=sem, vmem_limit_bytes=VMEM_LIMIT)


def _tile(n, pref):
    t = min(n, pref)
    while n % t:
        t //= 2
    return t


HI_HALF_MASK = -65536


def _pack_halves(lo, hi):
    lo_bits = lax.bitcast_convert_type(lo.astype(BF16).astype(F32), jnp.int32)
    hi_bits = lax.bitcast_convert_type(hi.astype(BF16).astype(F32), jnp.int32)
    return lax.shift_right_logical(lo_bits, 16) | (hi_bits & HI_HALF_MASK)


def _unpack_lo(w):
    return lax.bitcast_convert_type(w << 16, F32)


def _unpack_hi(w):
    return lax.bitcast_convert_type(w & HI_HALF_MASK, F32)


def _layer_norm_rows(x, g_ref, b_ref):
    mu = jnp.mean(x, axis=-1, keepdims=True)
    xc = x - mu
    var = jnp.mean(xc * xc, axis=-1, keepdims=True)
    return xc * lax.rsqrt(var + LN_EPS) * g_ref[...] + b_ref[...]


def _ln_kernel(alpha, n_add, packed, *refs):
    x_ref = refs[0]
    add_refs = refs[1:1 + n_add]
    g_ref, b_ref, of_ref, o2_ref = refs[1 + n_add:]
    x = x_ref[...]
    if n_add:
        x = alpha * x
        for a in add_refs:
            x = x + a[...].astype(F32)
    y = _layer_norm_rows(x, g_ref, b_ref)
    of_ref[...] = y
    if packed:
        half = y.shape[1] // 2
        o2_ref[...] = _pack_halves(y[:, :half], y[:, half:])
    else:
        o2_ref[...] = y.astype(BF16)


def _layer_norm(x, adds, alpha, g, b, packed=False):
    n, d = x.shape
    tr = _tile(n, 256)
    row = pl.BlockSpec((tr, d), lambda i: (i, 0))
    vec = pl.BlockSpec((1, d), lambda i: (0, 0))
    second = (jax.ShapeDtypeStruct((n, d // 2), jnp.int32) if packed else jax.ShapeDtypeStruct((n, d), BF16))
    return pl.pallas_call(
        functools.partial(_ln_kernel, alpha, len(adds), packed),
        out_shape=(jax.ShapeDtypeStruct((n, d), F32), second),
        grid=(n // tr,),
        in_specs=[row] * (1 + len(adds)) + [vec, vec],
        out_specs=(row, pl.BlockSpec((tr, second.shape[1]), lambda i: (i, 0))),
        compiler_params=_cparams(("arbitrary",)),
        name="layer_norm",
    )(x, *adds, g.reshape(1, d), b.reshape(1, d))


def _mm_kernel(ksizes, *refs):
    a_refs = refs[:len(ksizes)]
    w_ref, o_ref = refs[len(ksizes):]
    acc = None
    off = 0
    for a_ref, kk in zip(a_refs, ksizes):
        part = jnp.dot(a_ref[...], w_ref[off:off + kk, :], preferred_element_type=F32)
        acc = part if acc is None else acc + part
        off += kk
    o_ref[...] = acc.astype(o_ref.dtype)


def _matmul(parts, w, layer, out_dtype):
    m = parts[0].shape[0]
    _, k, n = w.shape
    ksizes = tuple(p.shape[1] for p in parts)
    assert sum(ksizes) == k
    tm = _tile(m, 512)
    tn = _tile(n, 1024)
    in_specs = [pl.BlockSpec((tm, kk), lambda j, i: (i, 0)) for kk in ksizes]
    in_specs.append(pl.BlockSpec((None, k, tn), lambda j, i: (layer, 0, j)))
    return pl.pallas_call(
        functools.partial(_mm_kernel, ksizes),
        out_shape=jax.ShapeDtypeStruct((m, n), out_dtype),
        grid=(n // tn, m // tm),
        in_specs=in_specs,
        out_specs=pl.BlockSpec((tm, tn), lambda j, i: (i, j)),
        compiler_params=_cparams(("arbitrary", "arbitrary")),
        name="dense_projection",
    )(*parts, w)


def _head_norm(o, gain, head_scale):
    ms = jnp.mean(o * o, axis=-1, keepdims=True)
    return o * lax.rsqrt(ms + RMS_EPS) * (gain * head_scale)


def _window_attn_kernel(slopes, dil, seq_l, blk, half, q_ref, kp_ref, kc_ref, kn_ref,
                        vp_ref, vc_ref, vn_ref, o_ref, lse_ref):
    lb = pl.program_id(2)
    ii = lax.broadcasted_iota(jnp.int32, (blk, 3 * blk), 0)
    jj = lax.broadcasted_iota(jnp.int32, (blk, 3 * blk), 1)
    dist = jnp.abs(blk + ii - jj)
    kpos = (lb - 1) * blk + jj
    valid = (dist <= half) & (kpos >= 0) & (kpos < seq_l)
    tok_dist = (dist * dil).astype(F32)
    lane = lax.broadcasted_iota(jnp.int32, (blk, HEAD_DIM), 1)
    lse_tile = jnp.zeros((blk, HEAD_DIM), F32)
    scale = HEAD_DIM ** -0.5
    for h in range(A_HEADS_PER_GROUP):
        hs = slice(h * HEAD_DIM, (h + 1) * HEAD_DIM)
        q = q_ref[:, hs]
        k = jnp.concatenate([kp_ref[:, hs], kc_ref[:, hs], kn_ref[:, hs]], axis=0)
        v = jnp.concatenate([vp_ref[:, hs], vc_ref[:, hs], vn_ref[:, hs]], axis=0)
        s = lax.dot_general(q, k, (((1,), (1,)), ((), ())), preferred_element_type=F32) * scale
        s = jnp.where(valid, s - slopes[h] * tok_dist, NEG_BIG)
        m = jnp.max(s, axis=-1, keepdims=True)
        p = jnp.exp(s - m)
        den = jnp.sum(p, axis=-1, keepdims=True)
        o = jnp.dot(p.astype(BF16), v, preferred_element_type=F32)
        o_ref[:, hs] = o / den
        lse_tile = jnp.where(lane == h, m + jnp.log(den), lse_tile)
    lse_ref[...] = lse_tile


A_GROUP_W = A_HEADS_PER_GROUP * HEAD_DIM
A_PARTS = 3


LANES = 128


def _dilate_kernel(dil, x_ref, o_ref, scr):
    rows = scr.shape[1] // dil
    for c in range(scr.shape[0]):
        scr[c] = x_ref[:, c * LANES:(c + 1) * LANES].astype(F32)
        for r in range(dil):
            c0 = r * A_GROUP_W + c * LANES
            o_ref[:, c0:c0 + LANES] = scr[c, pl.ds(r, rows, stride=dil), :].astype(BF16)


def _dilate(proj, group):
    _, dil = DIL_GROUPS[group]
    n = proj.shape[0]
    tm = _tile(n, 256)
    return pl.pallas_call(
        functools.partial(_dilate_kernel, dil),
        out_shape=jax.ShapeDtypeStruct((n // dil, A_PARTS * dil * A_GROUP_W), BF16),
        grid=(n // tm, A_PARTS),
        in_specs=[pl.BlockSpec((tm, A_GROUP_W), lambda i, p: (i, p * (A_W // A_GROUP_W) + group))],
        out_specs=pl.BlockSpec((tm // dil, dil * A_GROUP_W), lambda i, p: (i, p)),
        scratch_shapes=[pltpu.VMEM((A_GROUP_W // LANES, tm, LANES), F32)],
        compiler_params=_cparams(("arbitrary", "arbitrary")),
        name="dilate_rows",
    )(proj)


def _window_attention(src, row0, nseq, seq_t, group):
    window, dil = DIL_GROUPS[group]
    half = (window // 2) // dil
    seq_l = seq_t // dil
    blk = _tile(seq_l, 128)
    nlb = seq_l // blk
    base = row0 // dil // blk
    cw = A_GROUP_W
    slopes = tuple(2.0 ** (-8.0 * (group * A_HEADS_PER_GROUP + h + 1) / A_HEADS)
                   for h in range(A_HEADS_PER_GROUP))

    def spec(part, shift):
        def imap(b, r, lb):
            nb = jnp.clip(lb + shift, 0, nlb - 1)
            col = part * (A_W // cw) + group if dil == 1 else part * dil + r
            return (base + b * nlb + nb, col)
        return pl.BlockSpec((blk, cw), imap)

    rows = nseq * seq_l
    return pl.pallas_call(
        functools.partial(_window_attn_kernel, slopes, dil, seq_l, blk, half),
        out_shape=(jax.ShapeDtypeStruct((rows, dil * cw), F32),
                   jax.ShapeDtypeStruct((rows, dil * HEAD_DIM), F32)),
        grid=(nseq, dil, nlb),
        in_specs=[spec(0, 0), spec(1, -1), spec(1, 0), spec(1, 1), spec(2, -1), spec(2, 0), spec(2, 1)],
        out_specs=(pl.BlockSpec((blk, cw), lambda b, r, lb: (b * nlb + lb, r)),
                   pl.BlockSpec((blk, HEAD_DIM), lambda b, r, lb: (b * nlb + lb, r))),
        compiler_params=_cparams(("arbitrary", "arbitrary", "arbitrary")),
        name="window_attention",
    )(src, src, src, src, src, src, src)


def _undilate(ref, dil, scr):
    if dil == 1:
        return ref[...]
    rows, w = ref.shape[0], ref.shape[1] // dil
    for c in range(w // LANES):
        for r in range(dil):
            c0 = r * w + c * LANES
            scr[c, pl.ds(r, rows, stride=dil), :] = ref[:, c0:c0 + LANES]
    return jnp.concatenate([scr[c] for c in range(w // LANES)], axis=1)


def _window_merge_kernel(o0_ref, o1_ref, o2_ref, l0_ref, l1_ref, l2_ref, gain_ref, out_ref,
                         so1, so2, sl1, sl2):
    dils = tuple(d for _, d in DIL_GROUPS)
    outs = tuple(_undilate(r, d, s) for r, d, s in zip((o0_ref, o1_ref, o2_ref), dils, (None, so1, so2)))
    lses = tuple(_undilate(r, d, s) for r, d, s in zip((l0_ref, l1_ref, l2_ref), dils, (None, sl1, sl2)))
    mx = jnp.maximum(jnp.maximum(lses[0], lses[1]), lses[2])
    es = [jnp.exp(l - mx) for l in lses]
    tot = es[0] + es[1] + es[2]
    for g in range(len(DIL_GROUPS)):
        alpha = es[g] / tot
        for h in range(A_HEADS_PER_GROUP):
            hs = slice(h * HEAD_DIM, (h + 1) * HEAD_DIM)
            c0 = (g * A_HEADS_PER_GROUP + h) * HEAD_DIM
            o = outs[g][:, hs] * alpha[:, h:h + 1]
            out_ref[:, c0:c0 + HEAD_DIM] = _head_norm(o, gain_ref[:, c0:c0 + HEAD_DIM], 1.0).astype(BF16)


def _window_merge(outs, lses, gain_a):
    n = outs[0].shape[0]
    tr = _tile(n, 256)
    dils = tuple(d for _, d in DIL_GROUPS)
    return pl.pallas_call(
        _window_merge_kernel,
        out_shape=jax.ShapeDtypeStruct((n, A_W), BF16),
        grid=(n // tr,),
        in_specs=[pl.BlockSpec((tr // d, d * A_GROUP_W), lambda i: (i, 0)) for d in dils]
        + [pl.BlockSpec((tr // d, d * HEAD_DIM), lambda i: (i, 0)) for d in dils]
        + [pl.BlockSpec((1, A_W), lambda i: (0, 0))],
        out_specs=pl.BlockSpec((tr, A_W), lambda i: (i, 0)),
        scratch_shapes=[pltpu.VMEM((A_GROUP_W // LANES, tr, LANES), F32)] * 2
        + [pltpu.VMEM((HEAD_DIM // LANES, tr, LANES), F32)] * 2,
        compiler_params=_cparams(("arbitrary",)),
        name="window_merge",
    )(*outs, *lses, gain_a)


LOG2E = math.log2(math.e)
ATTN_KV_CHUNK = 1024
DIFF_Q_BLOCK = 512
GQA_Q_BLOCK = 256


def _flash_loop(q, k_ref, v_ref, sa_ref, sb_ref, m_ref, acc_ref, tkc, parts, bias_fn):
    n_chunks = k_ref.shape[0] // tkc
    ones = jnp.ones((tkc, HEAD_DIM), BF16)
    m_ref[...] = jnp.full(m_ref.shape, -jnp.inf, F32)
    acc_ref[...] = jnp.zeros(acc_ref.shape, F32)

    def scores(c):
        ks = pl.ds(pl.multiple_of(c * tkc, tkc), tkc)
        return lax.dot_general(q, k_ref[ks, :], (((1,), (1,)), ((), ())), preferred_element_type=F32)

    def consume(s_ref, c):
        ks = pl.ds(pl.multiple_of(c * tkc, tkc), tkc)
        v_ext = jnp.concatenate([v_ref[ks, :], ones], axis=1)
        bias = bias_fn(c)
        for r0, r1 in parts:
            u = s_ref[r0:r1, :] if bias is None else s_ref[r0:r1, :] + bias
            m_prev = m_ref[r0:r1, :]
            m_new = jnp.maximum(m_prev, jnp.max(u, axis=-1, keepdims=True))
            p = jnp.exp2((u - m_new).astype(BF16))
            alpha = jnp.exp2(m_prev - m_new)
            acc_ref[r0:r1, :] = alpha * acc_ref[r0:r1, :] + jnp.dot(p, v_ext, preferred_element_type=F32)
            m_ref[r0:r1, :] = m_new

    sa_ref[...] = scores(0)

    def body(i, carry):
        c = 2 * i
        sb_ref[...] = scores(c + 1)
        consume(sa_ref, c)
        sa_ref[...] = scores(jnp.minimum(c + 2, n_chunks - 1))
        consume(sb_ref, c + 1)
        return carry

    lax.fori_loop(0, n_chunks // 2, body, 0)


def _softmax_out(acc_ref, rows):
    acc = acc_ref[rows, :]
    return acc[:, :HEAD_DIM] / acc[:, HEAD_DIM:]


def _diff_attn_kernel(tq, tkc, head_scale, slope_ref, lam_ref, gain_ref, q_ref, k_ref, v_ref, o_ref,
                      sa_ref, sb_ref, m_ref, acc_ref):
    q0 = pl.program_id(2) * tq
    q = q_ref[...]
    lane = lax.broadcasted_iota(jnp.int32, q.shape, 1)
    zero = jnp.zeros_like(q)
    q = jnp.concatenate([jnp.where(lane < B_QK_DIM, q, zero), jnp.where(lane >= B_QK_DIM, q, zero)], axis=0)
    slope2 = slope_ref[pl.program_id(1)]
    dij = (lax.broadcasted_iota(jnp.int32, (tq, tkc), 0)
           - lax.broadcasted_iota(jnp.int32, (tq, tkc), 1)).astype(F32)

    def bias_fn(c):
        d = (q0 - c * tkc).astype(F32)
        return -slope2 * jnp.abs(dij + d)

    _flash_loop(q, k_ref, v_ref, sa_ref, sb_ref, m_ref, acc_ref, tkc, ((0, tq), (tq, 2 * tq)), bias_fn)
    o = _softmax_out(acc_ref, slice(0, tq)) - lam_ref[0] * _softmax_out(acc_ref, slice(tq, 2 * tq))
    o_ref[...] = _head_norm(o, gain_ref[...], head_scale).astype(BF16)


def _kv_chunk(seq_t):
    return _tile(seq_t // 2, ATTN_KV_CHUNK)


def _diff_attention(proj, row0, nseq, seq_t, lam, lam_init, gain_b):
    tq = _tile(seq_t, DIFF_Q_BLOCK)
    tkc = _kv_chunk(seq_t)
    nq = seq_t // tq
    slopes2 = jnp.asarray([2.0 ** (-8.0 * (h + 1) / B_HEADS) * LOG2E for h in range(B_HEADS)], F32)
    smem = pl.BlockSpec(memory_space=pltpu.SMEM)
    qcol, kcol, vcol = OFF_QB // HEAD_DIM, OFF_KB // HEAD_DIM, OFF_VB // HEAD_DIM
    return pl.pallas_call(
        functools.partial(_diff_attn_kernel, tq, tkc, 1.0 - lam_init),
        out_shape=jax.ShapeDtypeStruct((nseq * seq_t, B_V_W), BF16),
        grid=(nseq, B_HEADS, nq),
        in_specs=[smem, smem,
                  pl.BlockSpec((1, HEAD_DIM), lambda b, h, qi: (0, h)),
                  pl.BlockSpec((tq, HEAD_DIM), lambda b, h, qi: (row0 // tq + b * nq + qi, qcol + h)),
                  pl.BlockSpec((seq_t, HEAD_DIM), lambda b, h, qi: (row0 // seq_t + b, kcol + h)),
                  pl.BlockSpec((seq_t, HEAD_DIM), lambda b, h, qi: (row0 // seq_t + b, vcol + h))],
        out_specs=pl.BlockSpec((tq, HEAD_DIM), lambda b, h, qi: (b * nq + qi, h)),
        scratch_shapes=[pltpu.VMEM((2 * tq, tkc), F32), pltpu.VMEM((2 * tq, tkc), F32),
                        pltpu.VMEM((2 * tq, 1), F32), pltpu.VMEM((2 * tq, 2 * HEAD_DIM), F32)],
        compiler_params=_cparams(("arbitrary", "arbitrary", "arbitrary")),
        name="diff_attention",
    )(slopes2, lam.reshape(1), gain_b, proj, proj, proj)


def _qk_prep_kernel(proj_ref, gain_ref, scale_ref, cos_ref, sin_ref, o_ref):
    j = pl.program_id(1)
    x = proj_ref[...].astype(F32)
    ms = jnp.mean(x * x, axis=-1, keepdims=True)
    y = x * lax.rsqrt(ms + RMS_EPS) * gain_ref[pl.ds(j, 1), :]
    lane = lax.broadcasted_iota(jnp.int32, y.shape, 1)
    partner = jnp.where((lane & 1) == 0, pltpu.roll(y, HEAD_DIM - 1, 1), pltpu.roll(y, 1, 1))
    o_ref[...] = ((y * cos_ref[...] + partner * sin_ref[...]) * scale_ref[pl.ds(j, 1), :]).astype(BF16)


def _qk_prep(proj, gains, scales, cos_t, sin_t):
    n = proj.shape[0]
    tr = _tile(n, 512)
    nh = C_HEADS + C_KV_HEADS
    c0 = OFF_QC // HEAD_DIM
    full = pl.BlockSpec((nh, HEAD_DIM), lambda i, j: (0, 0))
    tab = pl.BlockSpec((tr, HEAD_DIM), lambda i, j: (i, 0))
    return pl.pallas_call(
        _qk_prep_kernel,
        out_shape=jax.ShapeDtypeStruct((n, nh * HEAD_DIM), BF16),
        grid=(n // tr, nh),
        in_specs=[pl.BlockSpec((tr, HEAD_DIM), lambda i, j: (i, c0 + j)), full, full, tab, tab],
        out_specs=pl.BlockSpec((tr, HEAD_DIM), lambda i, j: (i, j)),
        compiler_params=_cparams(("arbitrary", "arbitrary")),
        name="qk_norm_rope",
    )(proj, gains, scales, cos_t, sin_t)


def _gqa_kernel(tq, tkc, gain_ref, q_ref, k_ref, v_ref, o_ref, sa_ref, sb_ref, m_ref, acc_ref):
    q = jnp.concatenate([q_ref[:, g * HEAD_DIM:(g + 1) * HEAD_DIM] for g in range(C_GROUP)], axis=0)
    _flash_loop(q, k_ref, v_ref, sa_ref, sb_ref, m_ref, acc_ref, tkc, ((0, C_GROUP * tq),), lambda c: None)
    for g in range(C_GROUP):
        gs = slice(g * HEAD_DIM, (g + 1) * HEAD_DIM)
        o = _softmax_out(acc_ref, slice(g * tq, (g + 1) * tq))
        o_ref[:, gs] = _head_norm(o, gain_ref[:, gs], 1.0).astype(BF16)


def _gqa_attention(qk_rot, proj, row0, nseq, seq_t, gain_c):
    tq = _tile(seq_t, GQA_Q_BLOCK)
    tkc = _kv_chunk(seq_t)
    nq = seq_t // tq
    gw = C_GROUP * HEAD_DIM
    vcol = OFF_VC // HEAD_DIM
    rows = C_GROUP * tq
    return pl.pallas_call(
        functools.partial(_gqa_kernel, tq, tkc),
        out_shape=jax.ShapeDtypeStruct((nseq * seq_t, C_Q_W), BF16),
        grid=(nseq, C_KV_HEADS, nq),
        in_specs=[pl.BlockSpec((1, gw), lambda b, h, qi: (0, h)),
                  pl.BlockSpec((tq, gw), lambda b, h, qi: (row0 // tq + b * nq + qi, h)),
                  pl.BlockSpec((seq_t, HEAD_DIM), lambda b, h, qi: (row0 // seq_t + b, C_HEADS + h)),
                  pl.BlockSpec((seq_t, HEAD_DIM), lambda b, h, qi: (row0 // seq_t + b, vcol + h))],
        out_specs=pl.BlockSpec((tq, gw), lambda b, h, qi: (b * nq + qi, h)),
        scratch_shapes=[pltpu.VMEM((rows, tkc), F32), pltpu.VMEM((rows, tkc), F32),
                        pltpu.VMEM((rows, 1), F32), pltpu.VMEM((rows, 2 * HEAD_DIM), F32)],
        compiler_params=_cparams(("arbitrary", "arbitrary", "arbitrary")),
        name="gqa_attention",
    )(gain_c, qk_rot, qk_rot, proj)


def _rope_tables(seq_t):
    rows = seq_t // GRID_W
    r = jnp.repeat(jnp.arange(rows, dtype=F32), GRID_W)
    c = jnp.tile(jnp.arange(GRID_W, dtype=F32), rows)
    half = HEAD_DIM // 2
    inv_freq = ROPE_THETA ** (-jnp.arange(0, half, 2, dtype=F32) / half)
    ang = jnp.concatenate([r[:, None] * inv_freq, c[:, None] * inv_freq], -1)
    cos = jnp.repeat(jnp.cos(ang), 2, axis=-1)
    sin = jnp.repeat(jnp.sin(ang), 2, axis=-1) * jnp.tile(jnp.asarray([-1.0, 1.0], F32), half)
    return cos, sin


def _first_argmax(v, lane, n):
    m = jnp.max(v, axis=-1, keepdims=True)
    idx = jnp.min(jnp.where(v == m, lane, float(n)), axis=-1, keepdims=True)
    return m, idx


def _router_kernel(n_exp, x_ref, wh_ref, wl_ref, bias_ref, idx_ref, wt_ref):
    x = x_ref[...]
    xh = x.astype(BF16)
    xl = (x - xh.astype(F32)).astype(BF16)
    logits = (jnp.dot(xh, wh_ref[...], preferred_element_type=F32)
              + jnp.dot(xh, wl_ref[...], preferred_element_type=F32)
              + jnp.dot(xl, wh_ref[...], preferred_element_type=F32))
    scores = jax.nn.sigmoid(logits)
    biased = scores + bias_ref[...]
    tr = x.shape[0]
    gsz = n_exp // N_GROUPS
    lane_i = lax.broadcasted_iota(jnp.int32, (tr, n_exp), 1)
    lane = lane_i.astype(F32)
    grp = (lane_i // gsz).astype(F32)
    neg = -jnp.inf
    gscore = jnp.zeros((tr, n_exp), F32)
    for g in range(N_GROUPS):
        in_g = grp == float(g)
        vg = jnp.where(in_g, biased, neg)
        m1, i1 = _first_argmax(vg, lane, n_exp)
        m2 = jnp.max(jnp.where(lane == i1, neg, vg), axis=-1, keepdims=True)
        gscore = jnp.where(in_g, m1 + m2, gscore)
    cand = jnp.full((tr, n_exp), neg, F32)
    for _ in range(TOPK_GROUPS):
        _, i = _first_argmax(gscore, lane, n_exp)
        pick = grp == jnp.floor(i * (1.0 / gsz))
        cand = jnp.where(pick, biased, cand)
        gscore = jnp.where(pick, neg, gscore)
    lane_o = lax.broadcasted_iota(jnp.int32, idx_ref.shape, 1)
    idx_out = jnp.zeros(idx_ref.shape, F32)
    wt_out = jnp.zeros(wt_ref.shape, F32)
    for kk in range(TOP_K):
        _, i = _first_argmax(cand, lane, n_exp)
        pick = lane == i
        sk = jnp.sum(jnp.where(pick, scores, 0.0), axis=-1, keepdims=True)
        cand = jnp.where(pick, neg, cand)
        idx_out = jnp.where(lane_o == kk, i, idx_out)
        wt_out = jnp.where(lane_o == kk, sk, wt_out)
    idx_ref[...] = idx_out.astype(jnp.int32)
    wt_ref[...] = wt_out /jnp.sum(wt_out, axis=-1, keepdims=True) * ROUTED_SCALE


def _router(x, router_w, router_bias):
    n, d = x.shape
    n_exp = router_w.shape[1]
    tr = _tile(n, 256)
    wh = router_w.astype(BF16)
    wl = (router_w - wh.astype(F32)).astype(BF16)
    full = pl.BlockSpec((d, n_exp), lambda i: (0, 0))
    out = pl.BlockSpec((tr, HEAD_DIM), lambda i: (i, 0))
    idx, wt = pl.pallas_call(
        functools.partial(_router_kernel, n_exp),
        out_shape=(jax.ShapeDtypeStruct((n, HEAD_DIM), jnp.int32), jax.ShapeDtypeStruct((n, HEAD_DIM), F32)),
        grid=(n // tr,),
        in_specs=[pl.BlockSpec((tr, d), lambda i: (i, 0)), full, full,
                  pl.BlockSpec((1, n_exp), lambda i: (0, 0))],
        out_specs=(out, out),
        compiler_params=_cparams(("arbitrary",)),
        name="moe_router",
    )(x, wh, wl, router_bias.reshape(1, n_exp).astype(F32))
    return idx[:, :TOP_K], wt[:, :TOP_K]


ROW_DMA_UNROLL = 8


def _row_dma(n_rows, copy, wait):
    def one(r, u):
        if wait:
            copy(r).wait()
        else:
            copy(r).start(priority=u % 2)

    def group(i, c):
        for u in range(ROW_DMA_UNROLL):
            one(i * ROW_DMA_UNROLL + u, u)
        return c

    def single(r, c):
        one(r, 0)
        return c

    n_groups = n_rows // ROW_DMA_UNROLL
    lax.fori_loop(0, n_groups, group, 0)
    if not (isinstance(n_rows, int) and n_rows % ROW_DMA_UNROLL == 0):
        lax.fori_loop(n_groups * ROW_DMA_UNROLL, n_rows, single, 0)


def _gate_up_kernel(gather, bm, be_ref, nused_ref, *refs):
    b = pl.program_id(0)
    kc = pl.program_id(1)
    nused = nused_ref[0]
    if gather:
        tok_ref, tok_next_ref, x_hbm, wg_ref, wu_ref, h_ref, g_acc, u_acc, xbuf, sem = refs
        slot = b % 2

        def copies(idx_ref, s):
            return lambda r: pltpu.make_async_copy(x_hbm.at[pl.ds(idx_ref[0, 0, r], 1), :],
                                                   xbuf.at[s, pl.ds(r, 1), :], sem.at[s])

        @pl.when((kc == 0) & (b == 0) & (nused > 0))
        def _():
            _row_dma(bm, copies(tok_ref, 0), wait=False)

        @pl.when((kc == 0) & (b + 1 < nused))
        def _():
            _row_dma(bm, copies(tok_next_ref, 1 - slot), wait=False)

        @pl.when((kc == 0) & (b < nused))
        def _():
            _row_dma(bm, copies(tok_ref, slot), wait=True)
    else:
        x_ref, wg_ref, wu_ref, h_ref, g_acc, u_acc = refs

    @pl.when(b < nused)
    def _():
        w = xbuf[slot] if gather else x_ref[...]
        x = lax.bitcast_convert_type(jnp.where(kc == 0, w << 16, w & HI_HALF_MASK), F32).astype(BF16)
        g = jnp.dot(x, wg_ref[0], preferred_element_type=F32)
        u = jnp.dot(x, wu_ref[0], preferred_element_type=F32)

        @pl.when(kc == 0)
        def _():
            g_acc[...] = g
            u_acc[...] = u

        @pl.when(kc == 1)
        def _():
            gg = g_acc[...] + g
            h_ref[...] = (gg * jax.nn.sigmoid(gg) * (u_acc[...] + u)).astype(BF16)

    @pl.when((b >= nused) & (kc == 1))
    def _():
        h_ref[...] = jnp.zeros(h_ref.shape, h_ref.dtype)


def _down_rows(h_ref, wlo_ref, whi_ref, rw_ref):
    h = h_ref[...]
    rw = rw_ref[...]
    return _pack_halves(jnp.dot(h, wlo_ref[0], preferred_element_type=F32) * rw,
                        jnp.dot(h, whi_ref[0], preferred_element_type=F32) * rw)


def _down_kernel(be_ref, nused_ref, h_ref, wlo_ref, whi_ref, rw_ref, o_ref):
    o_ref[...] = _down_rows(h_ref, wlo_ref, whi_ref, rw_ref)


def _down_scatter_kernel(be_ref, nused_ref, nreal_ref, h_ref, wlo_ref, whi_ref, rw_ref, dst_ref, dst_prev_ref,
                         y_hbm, obuf, sem):
    b = pl.program_id(0)
    nused = nused_ref[0]
    slot = b % 2
    n_cur = nreal_ref[b]
    n_prev = nreal_ref[jnp.maximum(b - 1, 0)]

    def copies(idx_ref, s):
        return lambda r: pltpu.make_async_copy(obuf.at[s, pl.ds(r, 1), :],
                                               y_hbm.at[pl.ds(idx_ref[0, 0, r], 1), :], sem.at[s])

    @pl.when(b < nused)
    def _():
        obuf[slot] = _down_rows(h_ref, wlo_ref, whi_ref, rw_ref)

    @pl.when((b >= 1) & (b - 1 < nused))
    def _():
        _row_dma(n_prev, copies(dst_prev_ref, 1 - slot), wait=True)

    @pl.when(b < nused)
    def _():
        _row_dma(n_cur, copies(dst_ref, slot), wait=False)

    @pl.when((b == pl.num_programs(0) - 1) & (b < nused))
    def _():
        _row_dma(n_cur, copies(dst_ref, slot), wait=True)


def _expert_ffn(xp, w_gate, w_up, w_down, row_w, block_e, n_used, bm, row_tok=None, row_dst=None, n_out=None):
    dh = xp.shape[1]
    ff = w_gate.shape[-1]
    rows = row_w.shape[0]
    n_blocks = rows // bm
    dispatch = row_tok is not None

    def live(b, nused):
        return jnp.minimum(b, nused[0] - 1)

    def kclamp(b, kc, nused):
        return jnp.where(b < nused[0], kc, 1)

    def idx_spec(shift):
        return pl.BlockSpec((1, 1, bm), lambda b, *_: (jnp.clip(b + shift, 0, n_blocks - 1), 0, 0),
                            memory_space=pltpu.SMEM)

    if dispatch:
        tok = row_tok.reshape(n_blocks, 1, bm)
        x_specs = [idx_spec(0), idx_spec(1), pl.BlockSpec(memory_space=pl.ANY)]
        x_args = (tok, tok, xp)
        x_scratch = [pltpu.VMEM((2, bm, dh), jnp.int32), pltpu.SemaphoreType.DMA((2,))]
    else:
        x_specs = [pl.BlockSpec((bm, dh), lambda b, kc, be, nused: (live(b, nused), 0))]
        x_args = (xp,)
        x_scratch = []
    hmid = pl.pallas_call(
        functools.partial(_gate_up_kernel, dispatch, bm),
        out_shape=jax.ShapeDtypeStruct((rows, ff), BF16),
        grid_spec=pltpu.PrefetchScalarGridSpec(
            num_scalar_prefetch=2,
            grid=(n_blocks, 2),
            in_specs=x_specs + [
                pl.BlockSpec((1, dh, ff), lambda b, kc, be, nused: (be[b], kclamp(b, kc, nused), 0)),
                pl.BlockSpec((1, dh, ff), lambda b, kc, be, nused: (be[b], kclamp(b, kc, nused), 0))],
            out_specs=pl.BlockSpec((bm, ff), lambda b, kc, be, nused: (b, 0)),
            scratch_shapes=[pltpu.VMEM((bm, ff), F32), pltpu.VMEM((bm, ff), F32)] + x_scratch,
        ),
        compiler_params=_cparams(("arbitrary", "arbitrary")),
        name="expert_gate_up",
    )(block_e, n_used, *x_args, w_gate, w_up)

    rw = row_w.reshape(rows, 1)
    if not dispatch:
        tn = _tile(dh, 1024)
        nn = dh // tn
        return pl.pallas_call(
            _down_kernel,
            out_shape=jax.ShapeDtypeStruct((rows, dh), jnp.int32),
            grid_spec=pltpu.PrefetchScalarGridSpec(
                num_scalar_prefetch=2,
                grid=(n_blocks, nn),
                in_specs=[pl.BlockSpec((bm, ff), lambda b, j, be, nused: (b, 0)),
                          pl.BlockSpec((1, ff, tn), lambda b, j, be, nused: (be[b], 0, j)),
                          pl.BlockSpec((1, ff, tn), lambda b, j, be, nused: (be[b], 0, nn + j)),
                          pl.BlockSpec((bm, 1), lambda b, j, be, nused: (b, 0))],
                out_specs=pl.BlockSpec((bm, tn), lambda b, j, be, nused: (b, j)),
            ),
            compiler_params=_cparams(("arbitrary", "arbitrary")),
            name="expert_down",
        )(block_e, n_used, hmid, w_down, w_down, rw)

    dst = row_dst.reshape(n_blocks, 1, bm)
    return pl.pallas_call(
        _down_scatter_kernel,
        out_shape=jax.ShapeDtypeStruct((n_out, dh), jnp.int32),
        grid_spec=pltpu.PrefetchScalarGridSpec(
            num_scalar_prefetch=3,
            grid=(n_blocks,),
            in_specs=[pl.BlockSpec((bm, ff), lambda b, be, nused, nreal: (live(b, nused), 0)),
                      pl.BlockSpec((1, ff, dh), lambda b, be, nused, nreal: (be[b], 0, 0)),
                      pl.BlockSpec((1, ff, dh), lambda b, be, nused, nreal: (be[b], 0, 1)),
                      pl.BlockSpec((bm, 1), lambda b, be, nused, nreal: (live(b, nused), 0)),
                      idx_spec(0), idx_spec(-1)],
            out_specs=pl.BlockSpec(memory_space=pl.ANY),
            scratch_shapes=[pltpu.VMEM((2, bm, dh), jnp.int32), pltpu.SemaphoreType.DMA((2,))],
        ),
        compiler_params=_cparams(("arbitrary",)),
        name="expert_down_scatter",
    )(block_e, n_used, jnp.sum((dst >= 0).astype(jnp.int32), axis=(1, 2)), hmid, w_down, w_down, rw, dst, dst)


def _combine_ln_kernel(alpha, x_ref, g_ref, b_ref, *refs):
    packed_refs, (of_ref, ob_ref) = refs[:-2], refs[-2:]
    lo = hi = None
    for ref in packed_refs:
        w = ref[...]
        lo = _unpack_lo(w) if lo is None else lo + _unpack_lo(w)
        hi = _unpack_hi(w) if hi is None else hi + _unpack_hi(w)
    x = alpha * x_ref[...] + jnp.concatenate([lo, hi], axis=1)
    y = _layer_norm_rows(x, g_ref, b_ref)
    of_ref[...] = y
    ob_ref[...] = y.astype(BF16)


def _combine_ln(x, shared, y_slots, alpha, g, b):
    n, d = x.shape
    tr = _tile(n, 128)
    nt = n // tr
    row = pl.BlockSpec((tr, d), lambda i: (i, 0))
    vec = pl.BlockSpec((1, d), lambda i: (0, 0))
    slots = [pl.BlockSpec((tr, d // 2), functools.partial(lambda k, i: (k * nt + i, 0), k)) for k in range(TOP_K)]
    return pl.pallas_call(
        functools.partial(_combine_ln_kernel, alpha),
        out_shape=(jax.ShapeDtypeStruct((n, d), F32), jax.ShapeDtypeStruct((n, d), BF16)),
        grid=(nt,),
        in_specs=[row, vec, vec, pl.BlockSpec((tr, d // 2), lambda i: (i, 0))] + slots,
        out_specs=(row, row),
        compiler_params=_cparams(("arbitrary",)),
        name="moe_combine_layer_norm",
    )(x, g.reshape(1, d), b.reshape(1, d), shared, *([y_slots] * TOP_K))


def _dispatch_plan(idx, wts, n_exp, bm):
    n = idx.shape[0]
    nk = n * TOP_K
    n_blocks = -(-nk // bm) + n_exp
    rows = n_blocks * bm
    flat_e = idx.reshape(-1)
    experts = jnp.arange(n_exp, dtype=jnp.int32)
    order = jnp.argsort(flat_e, stable=True).astype(jnp.int32)
    counts = jnp.sum((experts[:, None] == flat_e[None, :]).astype(jnp.int32), axis=1)
    padded = (counts + bm - 1) // bm * bm
    starts = jnp.cumsum(counts) - counts
    pends = jnp.cumsum(padded)
    pstarts = pends - padded
    blk_start = jnp.arange(n_blocks, dtype=jnp.int32) * bm
    block_e = jnp.minimum(jnp.sum((pends[None, :] <= blk_start[:, None]).astype(jnp.int32), axis=1), n_exp - 1)
    n_used = (pends[-1] // bm).astype(jnp.int32).reshape(1)
    row = jnp.arange(rows, dtype=jnp.int32)
    row_e = jnp.repeat(block_e, bm)
    off = row - pstarts[row_e]
    valid = off < counts[row_e]
    pair = order[jnp.clip(starts[row_e] + off, 0, nk - 1)]
    row_tok = jnp.where(valid, pair // TOP_K, 0)
    row_w = jnp.where(valid, wts.reshape(-1)[pair], 0.0)
    row_dst = jnp.where(valid, (pair % TOP_K) * n + pair // TOP_K, -1)
    return row_tok, row_w, row_dst, block_e, n_used


def _token_mixer(xb, seqs, layer, lam_init, w_in, w_out, out_gain_l, qk_gains, qk_scales, lam, cos_t, sin_t):
    proj = _matmul([xb], w_in, layer, BF16)
    qk_rot = _qk_prep(proj, qk_gains, qk_scales, cos_t, sin_t)
    window_src = [proj if dil == 1 else _dilate(proj, g) for g, (_, dil) in enumerate(DIL_GROUPS)]
    gain = out_gain_l.reshape(1, MIX_WIDTH).astype(F32)
    gain_a, gain_b, gain_c = gain[:, :A_W], gain[:, A_W:A_W + B_V_W], gain[:, A_W + B_V_W:]
    a_parts, b_parts, c_parts = [], [], []
    for row0, nseq, seq_t in seqs:
        outs, lses = zip(*[_window_attention(window_src[g], row0, nseq, seq_t, g)
                           for g in range(len(DIL_GROUPS))])
        a_parts.append(_window_merge(outs, lses, gain_a))
        b_parts.append(_diff_attention(proj, row0, nseq, seq_t, lam, lam_init, gain_b))
        c_parts.append(_gqa_attention(qk_rot, proj, row0, nseq, seq_t, gain_c))
    parts = [jnp.concatenate(p, axis=0) for p in (a_parts, b_parts, c_parts)]
    return _matmul(parts, w_out, layer, F32)


def _moe(xf, xp, layer, router_w, router_bias, w_gate, w_up, w_down, sh_gate, sh_up, sh_down, bm):
    n = xf.shape[0]
    n_exp = router_w.shape[1]
    idx, wts = _router(xf, router_w, router_bias)
    row_tok, row_w, row_dst, block_e, n_used = _dispatch_plan(idx, wts, n_exp, bm)
    y_slots = _expert_ffn(xp, w_gate, w_up, w_down, row_w, block_e + layer * n_exp, n_used, bm,
                          row_tok=row_tok, row_dst=row_dst, n_out=TOP_K * n)
    sb = _tile(n, bm)
    shared = _expert_ffn(xp, sh_gate, sh_up, sh_down, jnp.ones((n,), F32),
                         jnp.full((n // sb,), layer, jnp.int32), jnp.full((1,), n // sb, jnp.int32), sb)
    return y_slots, shared


def _forward(x_prompt, x_sample, ln_in_g, ln_in_b, w_in, w_out, out_gain, q_norm, k_norm,
             lambda_q1, lambda_k1, lambda_q2, lambda_k2, ln1_g, ln1_b, router_w, router_bias,
             w_gate, w_up, w_down, shared_gate, shared_up, shared_down, ln2_g, ln2_b, moe_block=512):
    depth = w_in.shape[0]
    d = x_prompt.shape[-1]
    alpha = (2 * depth) ** 0.25
    bp, tp, _ = x_prompt.shape
    bs, ts, _ = x_sample.shape
    n_s = bs * ts
    assert ts >= tp and n_s % tp == 0
    seqs = ((0, bs, ts), (n_s, bp, tp))
    x = jnp.concatenate([x_sample.reshape(n_s, d), x_prompt.reshape(bp * tp, d)], axis=0)

    tabs = [_rope_tables(t) for t in (ts, tp)]
    cos_t = jnp.concatenate([jnp.tile(tabs[0][0], (bs, 1)), jnp.tile(tabs[1][0], (bp, 1))], axis=0)
    sin_t = jnp.concatenate([jnp.tile(tabs[0][1], (bs, 1)), jnp.tile(tabs[1][1], (bp, 1))], axis=0)
    qk_scales = jnp.concatenate([jnp.full((C_HEADS, HEAD_DIM), HEAD_DIM ** -0.5 * LOG2E, F32),
                                 jnp.ones((C_KV_HEADS, HEAD_DIM), F32)], axis=0)
    in_col_scale = jnp.ones((IN_WIDTH,), F32).at[OFF_QB:OFF_KB].set(B_QK_DIM ** -0.5 * LOG2E)

    w_in_b = (w_in * in_col_scale).astype(BF16)
    w_out_b = w_out.astype(BF16)
    expert_w = [w.astype(BF16).reshape((depth * w.shape[1],) + w.shape[2:]) for w in (w_gate, w_up, w_down)]
    shared_w = [w.astype(BF16) for w in (shared_gate, shared_up, shared_down)]

    xf, xb = _layer_norm(x, [], 1.0, ln_in_g, ln_in_b)
    for l in range(depth):
        lam_init = 0.8 - 0.6 * math.exp(-0.3 * l)
        lam = (jnp.exp(jnp.sum(lambda_q1[l].astype(F32) * lambda_k1[l].astype(F32)))
               - jnp.exp(jnp.sum(lambda_q2[l].astype(F32) * lambda_k2[l].astype(F32))) + lam_init)
        qk_gains = jnp.concatenate([jnp.tile(q_norm[l][None].astype(F32), (C_HEADS, 1)),
                                    jnp.tile(k_norm[l][None].astype(F32), (C_KV_HEADS, 1))], axis=0)
        mix = _token_mixer(xb, seqs, l, lam_init, w_in_b, w_out_b, out_gain[l],
                           qk_gains, qk_scales, lam, cos_t, sin_t)
        xf, xp = _layer_norm(xf, [mix], alpha, ln1_g[l], ln1_b[l], packed=True)
        y_slots, shared = _moe(xf, xp, l, router_w[l], router_bias[l], *expert_w, *shared_w, moe_block)
        xf, xb = _combine_ln(xf, shared, y_slots, alpha, ln2_g[l], ln2_b[l])
    return xf[n_s:].reshape(bp, tp, d), xf[:n_s].reshape(bs, ts, d)


def kernel(x_prompt, x_sample, ln_in_g, ln_in_b, w_in, w_out, out_gain, q_norm, k_norm, lambda_q1, lambda_k1,
           lambda_q2, lambda_k2, ln1_g, ln1_b, router_w, router_bias, w_gate, w_up, w_down, shared_gate,
           shared_up, shared_down, ln2_g, ln2_b):
    return _forward(x_prompt, x_sample, ln_in_g, ln_in_b, w_in, w_out, out_gain, q_norm, k_norm,
                    lambda_q1, lambda_k1, lambda_q2, lambda_k2, ln1_g, ln1_b, router_w, router_bias,
                    w_gate, w_up, w_down, shared_gate, shared_up, shared_down, ln2_g, ln2_b)
```

```python
import functools
import math

import jax
import jax.numpy as jnp
from jax import lax
from jax.experimental import pallas as pl
from jax.experimental.pallas import tpu as pltpu

F32 = jnp.float32
BF16 = jnp.bfloat16

HEAD_DIM = 128
GRID_W = 64
DIL_GROUPS = ((128, 1), (512, 4), (2048, 16))
A_HEADS_PER_GROUP = 4
A_HEADS = 12
B_HEADS = 8
B_QK_DIM = 64
C_HEADS = 12
C_KV_HEADS = 4
C_GROUP = C_HEADS // C_KV_HEADS
ROPE_THETA = 10000.0
A_W = A_HEADS * HEAD_DIM
B_QK_W = B_HEADS * 2 * B_QK_DIM
B_V_W = B_HEADS * HEAD_DIM
C_Q_W = C_HEADS * HEAD_DIM
C_KV_W = C_KV_HEADS * HEAD_DIM
IN_WIDTH = 3 * A_W + 2 * B_QK_W + B_V_W + C_Q_W + 2 * C_KV_W
MIX_WIDTH = (A_HEADS + B_HEADS + C_HEADS) * HEAD_DIM
OFF_QA, OFF_KA, OFF_VA = 0, A_W, 2 * A_W
OFF_QB = 3 * A_W
OFF_KB = OFF_QB + B_QK_W
OFF_VB = OFF_KB + B_QK_W
OFF_QC = OFF_VB + B_V_W
OFF_KC = OFF_QC + C_Q_W
OFF_VC = OFF_KC + C_KV_W
TOP_K = 8
N_GROUPS = 8
TOPK_GROUPS = 4
ROUTED_SCALE = 2.5
LN_EPS = 1e-5
RMS_EPS = 1e-6
NEG_BIG = -1e30

V7X_VMEM_BYTES = 64 * 1024 * 1024
VMEM_LIMIT = 52 * 1024 * 1024


def _cparams(sem):
    return pltpu.CompilerParams(dimension_semantics=sem, vmem_limit_bytes=VMEM_LIMIT)


def _tile(n, pref):
    t = min(n, pref)
    while n % t:
        t //= 2
    return t


HI_HALF_MASK = -65536


def _pack_halves(lo, hi):
    lo_bits = lax.bitcast_convert_type(lo.astype(BF16).astype(F32), jnp.int32)
    hi_bits = lax.bitcast_convert_type(hi.astype(BF16).astype(F32), jnp.int32)
    return lax.shift_right_logical(lo_bits, 16) | (hi_bits & HI_HALF_MASK)


def _unpack_lo(w):
    return lax.bitcast_convert_type(w << 16, F32)


def _unpack_hi(w):
    return lax.bitcast_convert_type(w & HI_HALF_MASK, F32)


def _layer_norm_rows(x, g_ref, b_ref):
    mu = jnp.mean(x, axis=-1, keepdims=True)
    xc = x - mu
    var = jnp.mean(xc * xc, axis=-1, keepdims=True)
    return xc * lax.rsqrt(var + LN_EPS) * g_ref[...] + b_ref[...]


def _ln_kernel(alpha, n_add, packed, *refs):
    x_ref = refs[0]
    add_refs = refs[1:1 + n_add]
    g_ref, b_ref, of_ref, o2_ref = refs[1 + n_add:]
    x = x_ref[...]
    if n_add:
        x = alpha * x
        for a in add_refs:
            x = x + a[...].astype(F32)
    y = _layer_norm_rows(x, g_ref, b_ref)
    of_ref[...] = y
    if packed:
        half = y.shape[1] // 2
        o2_ref[...] = _pack_halves(y[:, :half], y[:, half:])
    else:
        o2_ref[...] = y.astype(BF16)


def _layer_norm(x, adds, alpha, g, b, packed=False):
    n, d = x.shape
    tr = _tile(n, 256)
    row = pl.BlockSpec((tr, d), lambda i: (i, 0))
    vec = pl.BlockSpec((1, d), lambda i: (0, 0))
    second = (jax.ShapeDtypeStruct((n, d // 2), jnp.int32) if packed else jax.ShapeDtypeStruct((n, d), BF16))
    return pl.pallas_call(
        functools.partial(_ln_kernel, alpha, len(adds), packed),
        out_shape=(jax.ShapeDtypeStruct((n, d), F32), second),
        grid=(n // tr,),
        in_specs=[row] * (1 + len(adds)) + [vec, vec],
        out_specs=(row, pl.BlockSpec((tr, second.shape[1]), lambda i: (i, 0))),
        compiler_params=_cparams(("arbitrary",)),
        name="layer_norm",
    )(x, *adds, g.reshape(1, d), b.reshape(1, d))


def _mm_kernel(ksizes, *refs):
    a_refs = refs[:len(ksizes)]
    w_ref, o_ref = refs[len(ksizes):]
    acc = None
    off = 0
    for a_ref, kk in zip(a_refs, ksizes):
        part = jnp.dot(a_ref[...], w_ref[off:off + kk, :], preferred_element_type=F32)
        acc = part if acc is None else acc + part
        off += kk
    o_ref[...] = acc.astype(o_ref.dtype)


def _matmul(parts, w, layer, out_dtype):
    m = parts[0].shape[0]
    _, k, n = w.shape
    ksizes = tuple(p.shape[1] for p in parts)
    assert sum(ksizes) == k
    tm = _tile(m, 512)
    tn = _tile(n, 1024)
    in_specs = [pl.BlockSpec((tm, kk), lambda j, i: (i, 0)) for kk in ksizes]
    in_specs.append(pl.BlockSpec((None, k, tn), lambda j, i: (layer, 0, j)))
    return pl.pallas_call(
        functools.partial(_mm_kernel, ksizes),
        out_shape=jax.ShapeDtypeStruct((m, n), out_dtype),
        grid=(n // tn, m // tm),
        in_specs=in_specs,
        out_specs=pl.BlockSpec((tm, tn), lambda j, i: (i, j)),
        compiler_params=_cparams(("arbitrary", "arbitrary")),
        name="dense_projection",
    )(*parts, w)


def _head_norm(o, gain, head_scale):
    ms = jnp.mean(o * o, axis=-1, keepdims=True)
    return o * lax.rsqrt(ms + RMS_EPS) * (gain * head_scale)


def _window_attn_kernel(slopes, dil, seq_l, blk, half, q_ref, kp_ref, kc_ref, kn_ref,
                        vp_ref, vc_ref, vn_ref, o_ref, lse_ref):
    lb = pl.program_id(2)
    ii = lax.broadcasted_iota(jnp.int32, (blk, 3 * blk), 0)
    jj = lax.broadcasted_iota(jnp.int32, (blk, 3 * blk), 1)
    dist = jnp.abs(blk + ii - jj)
    kpos = (lb - 1) * blk + jj
    valid = (dist <= half) & (kpos >= 0) & (kpos < seq_l)
    tok_dist = (dist * dil).astype(F32)
    lane = lax.broadcasted_iota(jnp.int32, (blk, HEAD_DIM), 1)
    lse_tile = jnp.zeros((blk, HEAD_DIM), F32)
    scale = HEAD_DIM ** -0.5
    for h in range(A_HEADS_PER_GROUP):
        hs = slice(h * HEAD_DIM, (h + 1) * HEAD_DIM)
        q = q_ref[:, hs]
        k = jnp.concatenate([kp_ref[:, hs], kc_ref[:, hs], kn_ref[:, hs]], axis=0)
        v = jnp.concatenate([vp_ref[:, hs], vc_ref[:, hs], vn_ref[:, hs]], axis=0)
        s = lax.dot_general(q, k, (((1,), (1,)), ((), ())), preferred_element_type=F32) * scale
        s = jnp.where(valid, s - slopes[h] * tok_dist, NEG_BIG)
        m = jnp.max(s, axis=-1, keepdims=True)
        p = jnp.exp(s - m)
        den = jnp.sum(p, axis=-1, keepdims=True)
        o = jnp.dot(p.astype(BF16), v, preferred_element_type=F32)
        o_ref[:, hs] = o / den
        lse_tile = jnp.where(lane == h, m + jnp.log(den), lse_tile)
    lse_ref[...] = lse_tile


A_GROUP_W = A_HEADS_PER_GROUP * HEAD_DIM
A_PARTS = 3


LANES = 128


def _dilate_kernel(dil, x_ref, o_ref, scr):
    rows = scr.shape[1] // dil
    for c in range(scr.shape[0]):
        scr[c] = x_ref[:, c * LANES:(c + 1) * LANES].astype(F32)
        for r in range(dil):
            c0 = r * A_GROUP_W + c * LANES
            o_ref[:, c0:c0 + LANES] = scr[c, pl.ds(r, rows, stride=dil), :].astype(BF16)


def _dilate(proj, group):
    _, dil = DIL_GROUPS[group]
    n = proj.shape[0]
    tm = _tile(n, 256)
    return pl.pallas_call(
        functools.partial(_dilate_kernel, dil),
        out_shape=jax.ShapeDtypeStruct((n // dil, A_PARTS * dil * A_GROUP_W), BF16),
        grid=(n // tm, A_PARTS),
        in_specs=[pl.BlockSpec((tm, A_GROUP_W), lambda i, p: (i, p * (A_W // A_GROUP_W) + group))],
        out_specs=pl.BlockSpec((tm // dil, dil * A_GROUP_W), lambda i, p: (i, p)),
        scratch_shapes=[pltpu.VMEM((A_GROUP_W // LANES, tm, LANES), F32)],
        compiler_params=_cparams(("arbitrary", "arbitrary")),
        name="dilate_rows",
    )(proj)


def _window_attention(src, row0, nseq, seq_t, group):
    window, dil = DIL_GROUPS[group]
    half = (window // 2) // dil
    seq_l = seq_t // dil
    blk = _tile(seq_l, 128)
    nlb = seq_l // blk
    base = row0 // dil // blk
    cw = A_GROUP_W
    slopes = tuple(2.0 ** (-8.0 * (group * A_HEADS_PER_GROUP + h + 1) / A_HEADS)
                   for h in range(A_HEADS_PER_GROUP))

    def spec(part, shift):
        def imap(b, r, lb):
            nb = jnp.clip(lb + shift, 0, nlb - 1)
            col = part * (A_W // cw) + group if dil == 1 else part * dil + r
            return (base + b * nlb + nb, col)
        return pl.BlockSpec((blk, cw), imap)

    rows = nseq * seq_l
    return pl.pallas_call(
        functools.partial(_window_attn_kernel, slopes, dil, seq_l, blk, half),
        out_shape=(jax.ShapeDtypeStruct((rows, dil * cw), F32),
                   jax.ShapeDtypeStruct((rows, dil * HEAD_DIM), F32)),
        grid=(nseq, dil, nlb),
        in_specs=[spec(0, 0), spec(1, -1), spec(1, 0), spec(1, 1), spec(2, -1), spec(2, 0), spec(2, 1)],
        out_specs=(pl.BlockSpec((blk, cw), lambda b, r, lb: (b * nlb + lb, r)),
                   pl.BlockSpec((blk, HEAD_DIM), lambda b, r, lb: (b * nlb + lb, r))),
        compiler_params=_cparams(("arbitrary", "arbitrary", "arbitrary")),
        name="window_attention",
    )(src, src, src, src, src, src, src)


def _undilate(ref, dil, scr):
    if dil == 1:
        return ref[...]
    rows, w = ref.shape[0], ref.shape[1] // dil
    for c in range(w // LANES):
        for r in range(dil):
            c0 = r * w + c * LANES
            scr[c, pl.ds(r, rows, stride=dil), :] = ref[:, c0:c0 + LANES]
    return jnp.concatenate([scr[c] for c in range(w // LANES)], axis=1)


def _window_merge_kernel(o0_ref, o1_ref, o2_ref, l0_ref, l1_ref, l2_ref, gain_ref, out_ref,
                         so1, so2, sl1, sl2):
    dils = tuple(d for _, d in DIL_GROUPS)
    outs = tuple(_undilate(r, d, s) for r, d, s in zip((o0_ref, o1_ref, o2_ref), dils, (None, so1, so2)))
    lses = tuple(_undilate(r, d, s) for r, d, s in zip((l0_ref, l1_ref, l2_ref), dils, (None, sl1, sl2)))
    mx = jnp.maximum(jnp.maximum(lses[0], lses[1]), lses[2])
    es = [jnp.exp(l - mx) for l in lses]
    tot = es[0] + es[1] + es[2]
    for g in range(len(DIL_GROUPS)):
        alpha = es[g] / tot
        for h in range(A_HEADS_PER_GROUP):
            hs = slice(h * HEAD_DIM, (h + 1) * HEAD_DIM)
            c0 = (g * A_HEADS_PER_GROUP + h) * HEAD_DIM
            o = outs[g][:, hs] * alpha[:, h:h + 1]
            out_ref[:, c0:c0 + HEAD_DIM] = _head_norm(o, gain_ref[:, c0:c0 + HEAD_DIM], 1.0).astype(BF16)


def _window_merge(outs, lses, gain_a):
    n = outs[0].shape[0]
    tr = _tile(n, 256)
    dils = tuple(d for _, d in DIL_GROUPS)
    return pl.pallas_call(
        _window_merge_kernel,
        out_shape=jax.ShapeDtypeStruct((n, A_W), BF16),
        grid=(n // tr,),
        in_specs=[pl.BlockSpec((tr // d, d * A_GROUP_W), lambda i: (i, 0)) for d in dils]
        + [pl.BlockSpec((tr // d, d * HEAD_DIM), lambda i: (i, 0)) for d in dils]
        + [pl.BlockSpec((1, A_W), lambda i: (0, 0))],
        out_specs=pl.BlockSpec((tr, A_W), lambda i: (i, 0)),
        scratch_shapes=[pltpu.VMEM((A_GROUP_W // LANES, tr, LANES), F32)] * 2
        + [pltpu.VMEM((HEAD_DIM // LANES, tr, LANES), F32)] * 2,
        compiler_params=_cparams(("arbitrary",)),
        name="window_merge",
    )(*outs, *lses, gain_a)


LOG2E = math.log2(math.e)
ATTN_KV_CHUNK = 1024
DIFF_Q_BLOCK = 512
GQA_Q_BLOCK = 512


def _flash_loop(q, k_ref, v_ref, sa_ref, sb_ref, m_ref, acc_ref, tkc, parts, bias_fn):
    n_chunks = k_ref.shape[0] // tkc
    ones = jnp.ones((tkc, HEAD_DIM), BF16)
    m_ref[...] = jnp.full(m_ref.shape, -jnp.inf, F32)
    acc_ref[...] = jnp.zeros(acc_ref.shape, F32)

    def scores(c):
        ks = pl.ds(pl.multiple_of(c * tkc, tkc), tkc)
        return lax.dot_general(q, k_ref[ks, :], (((1,), (1,)), ((), ())), preferred_element_type=F32)

    def consume(s_ref, c):
        ks = pl.ds(pl.multiple_of(c * tkc, tkc), tkc)
        v_ext = jnp.concatenate([v_ref[ks, :], ones], axis=1)
        bias = bias_fn(c)
        for r0, r1 in parts:
            u = s_ref[r0:r1, :] if bias is None else s_ref[r0:r1, :] + bias
            m_prev = m_ref[r0:r1, :]
            m_new = jnp.maximum(m_prev, jnp.max(u, axis=-1, keepdims=True))
            p = jnp.exp2((u - m_new).astype(BF16))
            alpha = jnp.exp2(m_prev - m_new)
            acc_ref[r0:r1, :] = alpha * acc_ref[r0:r1, :] + jnp.dot(p, v_ext, preferred_element_type=F32)
            m_ref[r0:r1, :] = m_new

    sa_ref[...] = scores(0)

    def body(i, carry):
        c = 2 * i
        sb_ref[...] = scores(c + 1)
        consume(sa_ref, c)
        sa_ref[...] = scores(jnp.minimum(c + 2, n_chunks - 1))
        consume(sb_ref, c + 1)
        return carry

    lax.fori_loop(0, n_chunks // 2, body, 0)


def _softmax_out(acc_ref, rows):
    acc = acc_ref[rows, :]
    return acc[:, :HEAD_DIM] / acc[:, HEAD_DIM:]


def _diff_attn_kernel(tq, tkc, head_scale, slope_ref, lam_ref, gain_ref, q_ref, k_ref, v_ref, o_ref,
                      sa_ref, sb_ref, m_ref, acc_ref):
    q0 = pl.program_id(2) * tq
    q = q_ref[...]
    lane = lax.broadcasted_iota(jnp.int32, q.shape, 1)
    zero = jnp.zeros_like(q)
    q = jnp.concatenate([jnp.where(lane < B_QK_DIM, q, zero), jnp.where(lane >= B_QK_DIM, q, zero)], axis=0)
    slope2 = slope_ref[pl.program_id(1)]
    dij = (lax.broadcasted_iota(jnp.int32, (tq, tkc), 0)
           - lax.broadcasted_iota(jnp.int32, (tq, tkc), 1)).astype(F32)

    def bias_fn(c):
        d = (q0 - c * tkc).astype(F32)
        return -slope2 * jnp.abs(dij + d)

    _flash_loop(q, k_ref, v_ref, sa_ref, sb_ref, m_ref, acc_ref, tkc, ((0, tq), (tq, 2 * tq)), bias_fn)
    o = _softmax_out(acc_ref, slice(0, tq)) - lam_ref[0] * _softmax_out(acc_ref, slice(tq, 2 * tq))
    o_ref[...] = _head_norm(o, gain_ref[...], head_scale).astype(BF16)


def _kv_chunk(seq_t):
    return _tile(seq_t // 2, ATTN_KV_CHUNK)


def _diff_attention(proj, row0, nseq, seq_t, lam, lam_init, gain_b):
    tq = _tile(seq_t, DIFF_Q_BLOCK)
    tkc = _kv_chunk(seq_t)
    nq = seq_t // tq
    slopes2 = jnp.asarray([2.0 ** (-8.0 * (h + 1) / B_HEADS) * LOG2E for h in range(B_HEADS)], F32)
    smem = pl.BlockSpec(memory_space=pltpu.SMEM)
    qcol, kcol, vcol = OFF_QB // HEAD_DIM, OFF_KB // HEAD_DIM, OFF_VB // HEAD_DIM
    return pl.pallas_call(
        functools.partial(_diff_attn_kernel, tq, tkc, 1.0 - lam_init),
        out_shape=jax.ShapeDtypeStruct((nseq * seq_t, B_V_W), BF16),
        grid=(nseq, B_HEADS, nq),
        in_specs=[smem, smem,
                  pl.BlockSpec((1, HEAD_DIM), lambda b, h, qi: (0, h)),
                  pl.BlockSpec((tq, HEAD_DIM), lambda b, h, qi: (row0 // tq + b * nq + qi, qcol + h)),
                  pl.BlockSpec((seq_t, HEAD_DIM), lambda b, h, qi: (row0 // seq_t + b, kcol + h)),
                  pl.BlockSpec((seq_t, HEAD_DIM), lambda b, h, qi: (row0 // seq_t + b, vcol + h))],
        out_specs=pl.BlockSpec((tq, HEAD_DIM), lambda b, h, qi: (b * nq + qi, h)),
        scratch_shapes=[pltpu.VMEM((2 * tq, tkc), F32), pltpu.VMEM((2 * tq, tkc), F32),
                        pltpu.VMEM((2 * tq, 1), F32), pltpu.VMEM((2 * tq, 2 * HEAD_DIM), F32)],
        compiler_params=_cparams(("arbitrary", "arbitrary", "arbitrary")),
        name="diff_attention",
    )(slopes2, lam.reshape(1), gain_b, proj, proj, proj)


def _qk_prep_kernel(proj_ref, gain_ref, scale_ref, cos_ref, sin_ref, o_ref):
    j = pl.program_id(1)
    x = proj_ref[...].astype(F32)
    ms = jnp.mean(x * x, axis=-1, keepdims=True)
    y = x * lax.rsqrt(ms + RMS_EPS) * gain_ref[pl.ds(j, 1), :]
    lane = lax.broadcasted_iota(jnp.int32, y.shape, 1)
    partner = jnp.where((lane & 1) == 0, pltpu.roll(y, HEAD_DIM - 1, 1), pltpu.roll(y, 1, 1))
    o_ref[...] = ((y * cos_ref[...] + partner * sin_ref[...]) * scale_ref[pl.ds(j, 1), :]).astype(BF16)


def _qk_prep(proj, gains, scales, cos_t, sin_t):
    n = proj.shape[0]
    tr = _tile(n, 512)
    nh = C_HEADS + C_KV_HEADS
    c0 = OFF_QC // HEAD_DIM
    full = pl.BlockSpec((nh, HEAD_DIM), lambda i, j: (0, 0))
    tab = pl.BlockSpec((tr, HEAD_DIM), lambda i, j: (i, 0))
    return pl.pallas_call(
        _qk_prep_kernel,
        out_shape=jax.ShapeDtypeStruct((n, nh * HEAD_DIM), BF16),
        grid=(n // tr, nh),
        in_specs=[pl.BlockSpec((tr, HEAD_DIM), lambda i, j: (i, c0 + j)), full, full, tab, tab],
        out_specs=pl.BlockSpec((tr, HEAD_DIM), lambda i, j: (i, j)),
        compiler_params=_cparams(("arbitrary", "arbitrary")),
        name="qk_norm_rope",
    )(proj, gains, scales, cos_t, sin_t)


def _gqa_kernel(tq, tkc, gain_ref, q_ref, k_ref, v_ref, o_ref, sa_ref, sb_ref, m_ref, acc_ref):
    q = jnp.concatenate([q_ref[:, g * HEAD_DIM:(g + 1) * HEAD_DIM] for g in range(C_GROUP)], axis=0)
    _flash_loop(q, k_ref, v_ref, sa_ref, sb_ref, m_ref, acc_ref, tkc, ((0, C_GROUP * tq),), lambda c: None)
    for g in range(C_GROUP):
        gs = slice(g * HEAD_DIM, (g + 1) * HEAD_DIM)
        o = _softmax_out(acc_ref, slice(g * tq, (g + 1) * tq))
        o_ref[:, gs] = _head_norm(o, gain_ref[:, gs], 1.0).astype(BF16)


def _gqa_attention(qk_rot, proj, row0, nseq, seq_t, gain_c):
    tq = _tile(seq_t, GQA_Q_BLOCK)
    tkc = _kv_chunk(seq_t)
    nq = seq_t // tq
    gw = C_GROUP * HEAD_DIM
    vcol = OFF_VC // HEAD_DIM
    rows = C_GROUP * tq
    return pl.pallas_call(
        functools.partial(_gqa_kernel, tq, tkc),
        out_shape=jax.ShapeDtypeStruct((nseq * seq_t, C_Q_W), BF16),
        grid=(nseq, C_KV_HEADS, nq),
        in_specs=[pl.BlockSpec((1, gw), lambda b, h, qi: (0, h)),
                  pl.BlockSpec((tq, gw), lambda b, h, qi: (row0 // tq + b * nq + qi, h)),
                  pl.BlockSpec((seq_t, HEAD_DIM), lambda b, h, qi: (row0 // seq_t + b, C_HEADS + h)),
                  pl.BlockSpec((seq_t, HEAD_DIM), lambda b, h, qi: (row0 // seq_t + b, vcol + h))],
        out_specs=pl.BlockSpec((tq, gw), lambda b, h, qi: (b * nq + qi, h)),
        scratch_shapes=[pltpu.VMEM((rows, tkc), F32), pltpu.VMEM((rows, tkc), F32),
                        pltpu.VMEM((rows, 1), F32), pltpu.VMEM((rows, 2 * HEAD_DIM), F32)],
        compiler_params=_cparams(("arbitrary", "arbitrary", "arbitrary")),
        name="gqa_attention",
    )(gain_c, qk_rot, qk_rot, proj)


def _rope_tables(seq_t):
    rows = seq_t // GRID_W
    r = jnp.repeat(jnp.arange(rows, dtype=F32), GRID_W)
    c = jnp.tile(jnp.arange(GRID_W, dtype=F32), rows)
    half = HEAD_DIM // 2
    inv_freq = ROPE_THETA ** (-jnp.arange(0, half, 2, dtype=F32) / half)
    ang = jnp.concatenate([r[:, None] * inv_freq, c[:, None] * inv_freq], -1)
    cos = jnp.repeat(jnp.cos(ang), 2, axis=-1)
    sin = jnp.repeat(jnp.sin(ang), 2, axis=-1) * jnp.tile(jnp.asarray([-1.0, 1.0], F32), half)
    return cos, sin


def _first_argmax(v, lane, n):
    m = jnp.max(v, axis=-1, keepdims=True)
    idx = jnp.min(jnp.where(v == m, lane, float(n)), axis=-1, keepdims=True)
    return m, idx


def _router_kernel(n_exp, x_ref, wh_ref, wl_ref, bias_ref, idx_ref, wt_ref, cnt_ref):
    x = x_ref[...]
    xh = x.astype(BF16)
    xl = (x - xh.astype(F32)).astype(BF16)
    logits = (jnp.dot(xh, wh_ref[...], preferred_element_type=F32)
              + jnp.dot(xh, wl_ref[...], preferred_element_type=F32)
              + jnp.dot(xl, wh_ref[...], preferred_element_type=F32))
    scores = jax.nn.sigmoid(logits)
    biased = scores + bias_ref[...]
    tr = x.shape[0]
    gsz = n_exp // N_GROUPS
    lane_i = lax.broadcasted_iota(jnp.int32, (tr, n_exp), 1)
    lane = lane_i.astype(F32)
    grp = (lane_i // gsz).astype(F32)
    neg = -jnp.inf
    gscore = jnp.zeros((tr, n_exp), F32)
    for g in range(N_GROUPS):
        in_g = grp == float(g)
        vg = jnp.where(in_g, biased, neg)
        m1, i1 = _first_argmax(vg, lane, n_exp)
        m2 = jnp.max(jnp.where(lane == i1, neg, vg), axis=-1, keepdims=True)
        gscore = jnp.where(in_g, m1 + m2, gscore)
    cand = jnp.full((tr, n_exp), neg, F32)
    for _ in range(TOPK_GROUPS):
        _, i = _first_argmax(gscore, lane, n_exp)
        pick = grp == jnp.floor(i * (1.0 / gsz))
        cand = jnp.where(pick, biased, cand)
        gscore = jnp.where(pick, neg, gscore)
    lane_o = lax.broadcasted_iota(jnp.int32, idx_ref.shape, 1)
    idx_out = jnp.zeros(idx_ref.shape, F32)
    wt_out = jnp.zeros(wt_ref.shape, F32)
    picked = jnp.zeros((tr, n_exp), F32)
    for kk in range(TOP_K):
        _, i = _first_argmax(cand, lane, n_exp)
        pick = lane == i
        sk = jnp.sum(jnp.where(pick, scores, 0.0), axis=-1, keepdims=True)
        cand = jnp.where(pick, neg, cand)
        picked = jnp.where(pick, 1.0, picked)
        idx_out = jnp.where(lane_o == kk, i, idx_out)
        wt_out = jnp.where(lane_o == kk, sk, wt_out)
    idx_ref[...] = idx_out.astype(jnp.int32)
    wt_ref[...] = wt_out / jnp.sum(wt_out, axis=-1, keepdims=True) * ROUTED_SCALE
    per_expert = jnp.sum(picked, axis=0, keepdims=True)
    per_expert = jnp.concatenate([per_expert, jnp.zeros((1, cnt_ref.shape[1] - n_exp), F32)], axis=1)
    cnt_ref[...] = jnp.broadcast_to(per_expert, cnt_ref.shape)


def _router(x, router_w, router_bias):
    n, d = x.shape
    n_exp = router_w.shape[1]
    tr = _tile(n, 256)
    wh = router_w.astype(BF16)
    wl = (router_w - wh.astype(F32)).astype(BF16)
    full = pl.BlockSpec((d, n_exp), lambda i: (0, 0))
    out = pl.BlockSpec((tr, HEAD_DIM), lambda i: (i, 0))
    sub = 8
    idx, wt, cnt = pl.pallas_call(
        functools.partial(_router_kernel, n_exp),
        out_shape=(jax.ShapeDtypeStruct((n, HEAD_DIM), jnp.int32), jax.ShapeDtypeStruct((n, HEAD_DIM), F32),
                   jax.ShapeDtypeStruct((n // tr * sub, LANES), F32)),
        grid=(n // tr,),
        in_specs=[pl.BlockSpec((tr, d), lambda i: (i, 0)), full, full,
                  pl.BlockSpec((1, n_exp), lambda i: (0, 0))],
        out_specs=(out, out, pl.BlockSpec((sub, LANES), lambda i: (i, 0))),
        compiler_params=_cparams(("arbitrary",)),
        name="moe_router",
    )(x, wh, wl, router_bias.reshape(1, n_exp).astype(F32))
    counts = jnp.sum(cnt.reshape(n // tr, sub, LANES)[:, 0, :n_exp], axis=0).astype(jnp.int32)
    return idx[:, :TOP_K], wt[:, :TOP_K], counts


ROW_DMA_UNROLL = 8


def _row_dma(n_rows, copy, wait):
    def one(r, u):
        if wait:
            copy(r).wait()
        else:
            copy(r).start(priority=u % 2)

    def group(i, c):
        for u in range(ROW_DMA_UNROLL):
            one(i * ROW_DMA_UNROLL + u, u)
        return c

    def single(r, c):
        one(r, 0)
        return c

    n_groups = n_rows // ROW_DMA_UNROLL
    lax.fori_loop(0, n_groups, group, 0)
    if not (isinstance(n_rows, int) and n_rows % ROW_DMA_UNROLL == 0):
        lax.fori_loop(n_groups * ROW_DMA_UNROLL, n_rows, single, 0)


def _gate_up_kernel(gather, bm, be_ref, nused_ref, *refs):
    b = pl.program_id(0)
    kc = pl.program_id(1)
    nused = nused_ref[0]
    if gather:
        tok_ref, tok_next_ref, x_hbm, wgu_ref, h_ref, gu_acc, xbuf, sem = refs
        slot = b % 2

        def copies(idx_ref, s):
            return lambda r: pltpu.make_async_copy(x_hbm.at[pl.ds(idx_ref[0, 0, r], 1), :],
                                                   xbuf.at[s, pl.ds(r, 1), :], sem.at[s])

        @pl.when((kc == 0) & (b == 0) & (nused > 0))
        def _():
            _row_dma(bm, copies(tok_ref, 0), wait=False)

        @pl.when((kc == 0) & (b + 1 < nused))
        def _():
            _row_dma(bm, copies(tok_next_ref, 1 - slot), wait=False)

        @pl.when((kc == 0) & (b < nused))
        def _():
            _row_dma(bm, copies(tok_ref, slot), wait=True)
    else:
        x_ref, wgu_ref, h_ref, gu_acc = refs

    @pl.when(b < nused)
    def _():
        w = xbuf[slot] if gather else x_ref[...]
        x = lax.bitcast_convert_type(jnp.where(kc == 0, w << 16, w & HI_HALF_MASK), F32).astype(BF16)
        gu = jnp.dot(x, wgu_ref[0], preferred_element_type=F32)

        @pl.when(kc == 0)
        def _():
            gu_acc[...] = gu

        @pl.when(kc == 1)
        def _():
            ff = h_ref.shape[1]
            tot = gu_acc[...] + gu
            g = tot[:, :ff]
            h_ref[...] = (g * jax.nn.sigmoid(g) * tot[:, ff:]).astype(BF16)

    @pl.when((b >= nused) & (kc == 1))
    def _():
        h_ref[...] = jnp.zeros(h_ref.shape, h_ref.dtype)


def _down_rows(h_ref, wlo_ref, whi_ref, rw_ref):
    h = h_ref[...]
    rw = rw_ref[...]
    return _pack_halves(jnp.dot(h, wlo_ref[0], preferred_element_type=F32) * rw,
                        jnp.dot(h, whi_ref[0], preferred_element_type=F32) * rw)


def _down_kernel(be_ref, nused_ref, h_ref, wlo_ref, whi_ref, rw_ref, o_ref):
    o_ref[...] = _down_rows(h_ref, wlo_ref, whi_ref, rw_ref)


def _down_scatter_kernel(be_ref, nused_ref, nreal_ref, h_ref, wlo_ref, whi_ref, rw_ref, dst_ref, dst_prev_ref,
                         y_hbm, obuf, sem):
    b = pl.program_id(0)
    nused = nused_ref[0]
    slot = b % 2
    n_cur = nreal_ref[b]
    n_prev = nreal_ref[jnp.maximum(b - 1, 0)]

    def copies(idx_ref, s):
        return lambda r: pltpu.make_async_copy(obuf.at[s, pl.ds(r, 1), :],
                                               y_hbm.at[pl.ds(idx_ref[0, 0, r], 1), :], sem.at[s])

    @pl.when(b < nused)
    def _():
        obuf[slot] = _down_rows(h_ref, wlo_ref, whi_ref, rw_ref)

    @pl.when((b >= 1) & (b - 1 < nused))
    def _():
        _row_dma(n_prev, copies(dst_prev_ref, 1 - slot), wait=True)

    @pl.when(b < nused)
    def _():
        _row_dma(n_cur, copies(dst_ref, slot), wait=False)

    @pl.when((b == pl.num_programs(0) - 1) & (b < nused))
    def _():
        _row_dma(n_cur, copies(dst_ref, slot), wait=True)


def _expert_ffn(xp, w_gate_up, w_down, row_w, block_e, n_used, bm, row_tok=None, row_dst=None, n_out=None):
    dh = xp.shape[1]
    ff = w_gate_up.shape[-1] // 2
    rows = row_w.shape[0]
    n_blocks = rows // bm
    dispatch = row_tok is not None

    def live(b, nused):
        return jnp.minimum(b, nused[0] - 1)

    def kclamp(b, kc, nused):
        return jnp.where(b < nused[0], kc, 1)

    def idx_spec(shift):
        return pl.BlockSpec((1, 1, bm), lambda b, *_: (jnp.clip(b + shift, 0, n_blocks - 1), 0, 0),
                            memory_space=pltpu.SMEM)

    if dispatch:
        tok = row_tok.reshape(n_blocks, 1, bm)
        x_specs = [idx_spec(0), idx_spec(1), pl.BlockSpec(memory_space=pl.ANY)]
        x_args = (tok, tok, xp)
        x_scratch = [pltpu.VMEM((2, bm, dh), jnp.int32), pltpu.SemaphoreType.DMA((2,))]
    else:
        x_specs = [pl.BlockSpec((bm, dh), lambda b, kc, be, nused: (live(b, nused), 0))]
        x_args = (xp,)
        x_scratch = []
    hmid = pl.pallas_call(
        functools.partial(_gate_up_kernel, dispatch, bm),
        out_shape=jax.ShapeDtypeStruct((rows, ff), BF16),
        grid_spec=pltpu.PrefetchScalarGridSpec(
            num_scalar_prefetch=2,
            grid=(n_blocks, 2),
            in_specs=x_specs + [
                pl.BlockSpec((1, dh, 2 * ff), lambda b, kc, be, nused: (be[b], kclamp(b, kc, nused), 0))],
            out_specs=pl.BlockSpec((bm, ff), lambda b, kc, be, nused: (b, 0)),
            scratch_shapes=[pltpu.VMEM((bm, 2 * ff), F32)] + x_scratch,
        ),
        compiler_params=_cparams(("arbitrary", "arbitrary")),
        name="expert_gate_up",
    )(block_e, n_used, *x_args, w_gate_up)

    rw = row_w.reshape(rows, 1)
    if not dispatch:
        tn = _tile(dh, 1024)
        nn = dh // tn
        return pl.pallas_call(
            _down_kernel,
            out_shape=jax.ShapeDtypeStruct((rows, dh), jnp.int32),
            grid_spec=pltpu.PrefetchScalarGridSpec(
                num_scalar_prefetch=2,
                grid=(n_blocks, nn),
                in_specs=[pl.BlockSpec((bm, ff), lambda b, j, be, nused: (b, 0)),
                          pl.BlockSpec((1, ff, tn), lambda b, j, be, nused: (be[b], 0, j)),
                          pl.BlockSpec((1, ff, tn), lambda b, j, be, nused: (be[b], 0, nn + j)),
                          pl.BlockSpec((bm, 1), lambda b, j, be, nused: (b, 0))],
                out_specs=pl.BlockSpec((bm, tn), lambda b, j, be, nused: (b, j)),
            ),
            compiler_params=_cparams(("arbitrary", "arbitrary")),
            name="expert_down",
        )(block_e, n_used, hmid, w_down, w_down, rw)

    dst = row_dst.reshape(n_blocks, 1, bm)
    return pl.pallas_call(
        _down_scatter_kernel,
        out_shape=jax.ShapeDtypeStruct((n_out, dh), jnp.int32),
        grid_spec=pltpu.PrefetchScalarGridSpec(
            num_scalar_prefetch=3,
            grid=(n_blocks,),
            in_specs=[pl.BlockSpec((bm, ff), lambda b, be, nused, nreal: (live(b, nused), 0)),
                      pl.BlockSpec((1, ff, dh), lambda b, be, nused, nreal: (be[b], 0, 0)),
                      pl.BlockSpec((1, ff, dh), lambda b, be, nused, nreal: (be[b], 0, 1)),
                      pl.BlockSpec((bm, 1), lambda b, be, nused, nreal: (live(b, nused), 0)),
                      idx_spec(0), idx_spec(-1)],
            out_specs=pl.BlockSpec(memory_space=pl.ANY),
            scratch_shapes=[pltpu.VMEM((2, bm, dh), jnp.int32), pltpu.SemaphoreType.DMA((2,))],
        ),
        compiler_params=_cparams(("arbitrary",)),
        name="expert_down_scatter",
    )(block_e, n_used, jnp.sum((dst >= 0).astype(jnp.int32), axis=(1, 2)), hmid, w_down, w_down, rw, dst, dst)


def _combine_ln_kernel(alpha, x_ref, g_ref, b_ref, *refs):
    packed_refs, (of_ref, ob_ref) = refs[:-2], refs[-2:]
    lo = hi = None
    for ref in packed_refs:
        w = ref[...]
        lo = _unpack_lo(w) if lo is None else lo + _unpack_lo(w)
        hi = _unpack_hi(w) if hi is None else hi + _unpack_hi(w)
    x = alpha * x_ref[...] + jnp.concatenate([lo, hi], axis=1)
    y = _layer_norm_rows(x, g_ref, b_ref)
    of_ref[...] = y
    ob_ref[...] = y.astype(BF16)


def _combine_ln(x, shared, y_slots, alpha, g, b):
    n, d = x.shape
    tr = _tile(n, 128)
    nt = n // tr
    row = pl.BlockSpec((tr, d), lambda i: (i, 0))
    vec = pl.BlockSpec((1, d), lambda i: (0, 0))
    slots = [pl.BlockSpec((tr, d // 2), functools.partial(lambda k, i: (k * nt + i, 0), k)) for k in range(TOP_K)]
    return pl.pallas_call(
        functools.partial(_combine_ln_kernel, alpha),
        out_shape=(jax.ShapeDtypeStruct((n, d), F32), jax.ShapeDtypeStruct((n, d), BF16)),
        grid=(nt,),
        in_specs=[row, vec, vec, pl.BlockSpec((tr, d // 2), lambda i: (i, 0))] + slots,
        out_specs=(row, row),
        compiler_params=_cparams(("arbitrary",)),
        name="moe_combine_layer_norm",
    )(x, g.reshape(1, d), b.reshape(1, d), shared, *([y_slots] * TOP_K))


def _dispatch_plan(idx, wts, counts, bm):
    n = idx.shape[0]
    n_exp = counts.shape[0]
    nk = n * TOP_K
    n_blocks = -(-nk // bm) + n_exp
    rows = n_blocks * bm
    flat_e = idx.reshape(-1)
    order = jnp.argsort(flat_e, stable=True).astype(jnp.int32)
    padded = (counts + bm - 1) // bm * bm
    starts = jnp.cumsum(counts) - counts
    pends = jnp.cumsum(padded)
    pstarts = pends - padded
    blk_start = jnp.arange(n_blocks, dtype=jnp.int32) * bm
    block_e = jnp.minimum(jnp.sum((pends[None, :] <= blk_start[:, None]).astype(jnp.int32), axis=1), n_exp - 1)
    n_used = (pends[-1] // bm).astype(jnp.int32).reshape(1)
    row = jnp.arange(rows, dtype=jnp.int32)
    row_e = jnp.repeat(block_e, bm)
    off = row - pstarts[row_e]
    valid = off < counts[row_e]
    pair = order[jnp.clip(starts[row_e] + off, 0, nk - 1)]
    row_tok = jnp.where(valid, pair // TOP_K, 0)
    row_w = jnp.where(valid, wts.reshape(-1)[pair], 0.0)
    row_dst = jnp.where(valid, (pair % TOP_K) * n + pair // TOP_K, -1)
    return row_tok, row_w, row_dst, block_e, n_used


def _token_mixer(xb, seqs, layer, lam_init, w_in, w_out, out_gain_l, qk_gains, qk_scales, lam, cos_t, sin_t):
    proj = _matmul([xb], w_in, layer, BF16)
    qk_rot = _qk_prep(proj, qk_gains, qk_scales, cos_t, sin_t)
    window_src = [proj if dil == 1 else _dilate(proj, g) for g, (_, dil) in enumerate(DIL_GROUPS)]
    gain = out_gain_l.reshape(1, MIX_WIDTH).astype(F32)
    gain_a, gain_b, gain_c = gain[:, :A_W], gain[:, A_W:A_W + B_V_W], gain[:, A_W + B_V_W:]
    a_parts, b_parts, c_parts = [], [], []
    for row0, nseq, seq_t in seqs:
        outs, lses = zip(*[_window_attention(window_src[g], row0, nseq, seq_t, g)
                           for g in range(len(DIL_GROUPS))])
        a_parts.append(_window_merge(outs, lses, gain_a))
        b_parts.append(_diff_attention(proj, row0, nseq, seq_t, lam, lam_init, gain_b))
        c_parts.append(_gqa_attention(qk_rot, proj, row0, nseq, seq_t, gain_c))
    parts = [jnp.concatenate(p, axis=0) for p in (a_parts, b_parts, c_parts)]
    return _matmul(parts, w_out, layer, F32)


def _moe(xf, xp, layer, router_w, router_bias, w_gate_up, w_down, sh_gate_up, sh_down, bm):
    n = xf.shape[0]
    n_exp = router_w.shape[1]
    idx, wts, counts = _router(xf, router_w, router_bias)
    row_tok, row_w, row_dst, block_e, n_used = _dispatch_plan(idx, wts, counts, bm)
    y_slots = _expert_ffn(xp, w_gate_up, w_down, row_w, block_e + layer * n_exp, n_used, bm,
                          row_tok=row_tok, row_dst=row_dst, n_out=TOP_K * n)
    sb = _tile(n, bm)
    shared = _expert_ffn(xp, sh_gate_up, sh_down, jnp.ones((n,), F32),
                         jnp.full((n // sb,), layer, jnp.int32), jnp.full((1,), n // sb, jnp.int32), sb)
    return y_slots, shared


def _forward(x_prompt, x_sample, ln_in_g, ln_in_b, w_in, w_out, out_gain, q_norm, k_norm,
             lambda_q1, lambda_k1, lambda_q2, lambda_k2, ln1_g, ln1_b, router_w, router_bias,
             w_gate, w_up, w_down, shared_gate, shared_up, shared_down, ln2_g, ln2_b, moe_block=512):
    depth = w_in.shape[0]
    d = x_prompt.shape[-1]
    alpha = (2 * depth) ** 0.25
    bp, tp, _ = x_prompt.shape
    bs, ts, _ = x_sample.shape
    n_s = bs * ts
    assert ts >= tp and n_s % tp == 0
    seqs = ((0, bs, ts), (n_s, bp, tp))
    x = jnp.concatenate([x_sample.reshape(n_s, d), x_prompt.reshape(bp * tp, d)], axis=0)

    tabs = [_rope_tables(t) for t in (ts, tp)]
    cos_t = jnp.concatenate([jnp.tile(tabs[0][0], (bs, 1)), jnp.tile(tabs[1][0], (bp, 1))], axis=0)
    sin_t = jnp.concatenate([jnp.tile(tabs[0][1], (bs, 1)), jnp.tile(tabs[1][1], (bp, 1))], axis=0)
    qk_scales = jnp.concatenate([jnp.full((C_HEADS, HEAD_DIM), HEAD_DIM ** -0.5 * LOG2E, F32),
                                 jnp.ones((C_KV_HEADS, HEAD_DIM), F32)], axis=0)
    in_col_scale = jnp.ones((IN_WIDTH,), F32).at[OFF_QB:OFF_KB].set(B_QK_DIM ** -0.5 * LOG2E)

    w_in_b = (w_in * in_col_scale).astype(BF16)
    w_out_b = w_out.astype(BF16)
    def stack_experts(w):
        return w.reshape((depth * w.shape[1],) + w.shape[2:])

    expert_w = [stack_experts(jnp.concatenate([w_gate.astype(BF16), w_up.astype(BF16)], axis=-1)),
                stack_experts(w_down.astype(BF16))]
    shared_w = [jnp.concatenate([shared_gate.astype(BF16), shared_up.astype(BF16)], axis=-1),
                shared_down.astype(BF16)]

    xf, xb = _layer_norm(x, [], 1.0, ln_in_g, ln_in_b)
    for l in range(depth):
        lam_init = 0.8 - 0.6 * math.exp(-0.3 * l)
        lam = (jnp.exp(jnp.sum(lambda_q1[l].astype(F32) * lambda_k1[l].astype(F32)))
               - jnp.exp(jnp.sum(lambda_q2[l].astype(F32) * lambda_k2[l].astype(F32))) + lam_init)
        qk_gains = jnp.concatenate([jnp.tile(q_norm[l][None].astype(F32), (C_HEADS, 1)),
                                    jnp.tile(k_norm[l][None].astype(F32), (C_KV_HEADS, 1))], axis=0)
        mix = _token_mixer(xb, seqs, l, lam_init, w_in_b, w_out_b, out_gain[l],
                           qk_gains, qk_scales, lam, cos_t, sin_t)
        xf, xp = _layer_norm(xf, [mix], alpha, ln1_g[l], ln1_b[l], packed=True)
        y_slots, shared = _moe(xf, xp, l, router_w[l], router_bias[l], *expert_w, *shared_w, moe_block)
        xf, xb = _combine_ln(xf, shared, y_slots, alpha, ln2_g[l], ln2_b[l])
    return xf[n_s:].reshape(bp, tp, d), xf[:n_s].reshape(bs, ts, d)


def kernel(x_prompt, x_sample, ln_in_g, ln_in_b, w_in, w_out, out_gain, q_norm, k_norm, lambda_q1, lambda_k1,
           lambda_q2, lambda_k2, ln1_g, ln1_b, router_w, router_bias, w_gate, w_up, w_down, shared_gate,
           shared_up, shared_down, ln2_g, ln2_b):
    return _forward(x_prompt, x_sample, ln_in_g, ln_in_b, w_in, w_out, out_gain, q_norm, k_norm,
                    lambda_q1, lambda_k1, lambda_q2, lambda_k2, ln1_g, ln1_b, router_w, router_bias,
                    w_gate, w_up, w_down, shared_gate, shared_up, shared_down, ln2_g, ln2_b)
```

```python
import functools
import math

import jax
import jax.numpy as jnp
from jax import lax
from jax.experimental import pallas as pl
from jax.experimental.pallas import tpu as pltpu

F32 = jnp.float32
BF16 = jnp.bfloat16

HEAD_DIM = 128
GRID_W = 64
DIL_GROUPS = ((128, 1), (512, 4), (2048, 16))
A_HEADS_PER_GROUP = 4
A_HEADS = 12
B_HEADS = 8
B_QK_DIM = 64
C_HEADS = 12
C_KV_HEADS = 4
C_GROUP = C_HEADS // C_KV_HEADS
ROPE_THETA = 10000.0
A_W = A_HEADS * HEAD_DIM
B_QK_W = B_HEADS * 2 * B_QK_DIM
B_V_W = B_HEADS * HEAD_DIM
C_Q_W = C_HEADS * HEAD_DIM
C_KV_W = C_KV_HEADS * HEAD_DIM
IN_WIDTH = 3 * A_W + 2 * B_QK_W + B_V_W + C_Q_W + 2 * C_KV_W
MIX_WIDTH = (A_HEADS + B_HEADS + C_HEADS) * HEAD_DIM
OFF_QA, OFF_KA, OFF_VA = 0, A_W, 2 * A_W
OFF_QB = 3 * A_W
OFF_KB = OFF_QB + B_QK_W
OFF_VB = OFF_KB + B_QK_W
OFF_QC = OFF_VB + B_V_W
OFF_KC = OFF_QC + C_Q_W
OFF_VC = OFF_KC + C_KV_W
TOP_K = 8
N_GROUPS = 8
TOPK_GROUPS = 4
ROUTED_SCALE = 2.5
LN_EPS = 1e-5
RMS_EPS = 1e-6
NEG_BIG = -1e30

V7X_VMEM_BYTES = 64 * 1024 * 1024
MXU_COLS = 256
VMEM_LIMIT = 52 * 1024 * 1024


def _cparams(sem):
    return pltpu.CompilerParams(dimension_semantics=sem, vmem_limit_bytes=VMEM_LIMIT)


def _tile(n, pref):
    t = min(n, pref)
    while n % t:
        t //= 2
    return t


HI_HALF_MASK = -65536


def _pack_halves(lo, hi):
    lo_bits = lax.bitcast_convert_type(lo.astype(BF16).astype(F32), jnp.int32)
    hi_bits = lax.bitcast_convert_type(hi.astype(BF16).astype(F32), jnp.int32)
    return lax.shift_right_logical(lo_bits, 16) | (hi_bits & HI_HALF_MASK)


def _unpack_lo(w):
    return lax.bitcast_convert_type(w << 16, F32)


def _unpack_hi(w):
    return lax.bitcast_convert_type(w & HI_HALF_MASK, F32)


def _layer_norm_rows(x, g_ref, b_ref):
    mu = jnp.mean(x, axis=-1, keepdims=True)
    xc = x - mu
    var = jnp.mean(xc * xc, axis=-1, keepdims=True)
    return xc * lax.rsqrt(var + LN_EPS) * g_ref[...] + b_ref[...]


def _ln_kernel(alpha, n_add, packed, *refs):
    x_ref = refs[0]
    add_refs = refs[1:1 + n_add]
    g_ref, b_ref, of_ref, o2_ref = refs[1 + n_add:]
    x = x_ref[...]
    if n_add:
        x = alpha * x
        for a in add_refs:
            x = x + a[...].astype(F32)
    y = _layer_norm_rows(x, g_ref, b_ref)
    of_ref[...] = y
    if packed:
        half = y.shape[1] // 2
        o2_ref[...] = _pack_halves(y[:, :half], y[:, half:])
    else:
        o2_ref[...] = y.astype(BF16)


def _layer_norm(x, adds, alpha, g, b, packed=False):
    n, d = x.shape
    tr = _tile(n, 256)
    row = pl.BlockSpec((tr, d), lambda i: (i, 0))
    vec = pl.BlockSpec((1, d), lambda i: (0, 0))
    second = (jax.ShapeDtypeStruct((n, d // 2), jnp.int32) if packed else jax.ShapeDtypeStruct((n, d), BF16))
    return pl.pallas_call(
        functools.partial(_ln_kernel, alpha, len(adds), packed),
        out_shape=(jax.ShapeDtypeStruct((n, d), F32), second),
        grid=(n // tr,),
        in_specs=[row] * (1 + len(adds)) + [vec, vec],
        out_specs=(row, pl.BlockSpec((tr, second.shape[1]), lambda i: (i, 0))),
        compiler_params=_cparams(("arbitrary",)),
        name="layer_norm",
    )(x, *adds, g.reshape(1, d), b.reshape(1, d))


def _mm_kernel(ksizes, *refs):
    a_refs = refs[:len(ksizes)]
    w_ref, o_ref = refs[len(ksizes):]
    acc = None
    off = 0
    for a_ref, kk in zip(a_refs, ksizes):
        part = jnp.dot(a_ref[...], w_ref[off:off + kk, :], preferred_element_type=F32)
        acc = part if acc is None else acc + part
        off += kk
    o_ref[...] = acc.astype(o_ref.dtype)


def _matmul(parts, w, layer, out_dtype):
    m = parts[0].shape[0]
    _, k, n = w.shape
    ksizes = tuple(p.shape[1] for p in parts)
    assert sum(ksizes) == k
    tm = _tile(m, 512)
    tn = _tile(n, 1024)
    in_specs = [pl.BlockSpec((tm, kk), lambda j, i: (i, 0)) for kk in ksizes]
    in_specs.append(pl.BlockSpec((None, k, tn), lambda j, i: (layer, 0, j)))
    return pl.pallas_call(
        functools.partial(_mm_kernel, ksizes),
        out_shape=jax.ShapeDtypeStruct((m, n), out_dtype),
        grid=(n // tn, m // tm),
        in_specs=in_specs,
        out_specs=pl.BlockSpec((tm, tn), lambda j, i: (i, j)),
        compiler_params=_cparams(("arbitrary", "arbitrary")),
        name="dense_projection",
    )(*parts, w)


def _head_norm(o, gain, head_scale):
    ms = jnp.mean(o * o, axis=-1, keepdims=True)
    return o * lax.rsqrt(ms + RMS_EPS) * (gain * head_scale)


def _window_attn_kernel(slopes, dil, seq_l, blk, half, q_ref, kp_ref, kc_ref, kn_ref,
                        vp_ref, vc_ref, vn_ref, o_ref, lse_ref):
    lb = pl.program_id(2)
    ii = lax.broadcasted_iota(jnp.int32, (blk, 3 * blk), 0)
    jj = lax.broadcasted_iota(jnp.int32, (blk, 3 * blk), 1)
    dist = jnp.abs(blk + ii - jj)
    kpos = (lb - 1) * blk + jj
    valid = (dist <= half) & (kpos >= 0) & (kpos < seq_l)
    tok_dist = (dist * dil).astype(F32)
    lane = lax.broadcasted_iota(jnp.int32, (blk, HEAD_DIM), 1)
    lse_tile = jnp.zeros((blk, HEAD_DIM), F32)
    scale = HEAD_DIM ** -0.5
    for h in range(A_HEADS_PER_GROUP):
        hs = slice(h * HEAD_DIM, (h + 1) * HEAD_DIM)
        q = q_ref[:, hs]
        k = jnp.concatenate([kp_ref[:, hs], kc_ref[:, hs], kn_ref[:, hs]], axis=0)
        v = jnp.concatenate([vp_ref[:, hs], vc_ref[:, hs], vn_ref[:, hs]], axis=0)
        s = lax.dot_general(q, k, (((1,), (1,)), ((), ())), preferred_element_type=F32) * scale
        s = jnp.where(valid, s - slopes[h] * tok_dist, NEG_BIG)
        m = jnp.max(s, axis=-1, keepdims=True)
        p = jnp.exp(s - m)
        den = jnp.sum(p, axis=-1, keepdims=True)
        o = jnp.dot(p.astype(BF16), v, preferred_element_type=F32)
        o_ref[:, hs] = o / den
        lse_tile = jnp.where(lane == h, m + jnp.log(den), lse_tile)
    lse_ref[...] = lse_tile


A_GROUP_W = A_HEADS_PER_GROUP * HEAD_DIM
A_PARTS = 3


LANES = 128


def _dilate_kernel(dil, x_ref, o_ref, scr):
    rows = scr.shape[1] // dil
    for c in range(scr.shape[0]):
        scr[c] = x_ref[:, c * LANES:(c + 1) * LANES].astype(F32)
        for r in range(dil):
            c0 = r * A_GROUP_W + c * LANES
            o_ref[:, c0:c0 + LANES] = scr[c, pl.ds(r, rows, stride=dil), :].astype(BF16)


def _dilate(proj, group):
    _, dil = DIL_GROUPS[group]
    n = proj.shape[0]
    tm = _tile(n, 256)
    return pl.pallas_call(
        functools.partial(_dilate_kernel, dil),
        out_shape=jax.ShapeDtypeStruct((n // dil, A_PARTS * dil * A_GROUP_W), BF16),
        grid=(n // tm, A_PARTS),
        in_specs=[pl.BlockSpec((tm, A_GROUP_W), lambda i, p: (i, p * (A_W // A_GROUP_W) + group))],
        out_specs=pl.BlockSpec((tm // dil, dil * A_GROUP_W), lambda i, p: (i, p)),
        scratch_shapes=[pltpu.VMEM((A_GROUP_W // LANES, tm, LANES), F32)],
        compiler_params=_cparams(("arbitrary", "arbitrary")),
        name="dilate_rows",
    )(proj)


def _window_attention(src, row0, nseq, seq_t, group):
    window, dil = DIL_GROUPS[group]
    half = (window // 2) // dil
    seq_l = seq_t // dil
    blk = _tile(seq_l, 128)
    nlb = seq_l // blk
    base = row0 // dil // blk
    cw = A_GROUP_W
    slopes = tuple(2.0 ** (-8.0 * (group * A_HEADS_PER_GROUP + h + 1) / A_HEADS)
                   for h in range(A_HEADS_PER_GROUP))

    def spec(part, shift):
        def imap(b, r, lb):
            nb = jnp.clip(lb + shift, 0, nlb - 1)
            col = part * (A_W // cw) + group if dil == 1 else part * dil + r
            return (base + b * nlb + nb, col)
        return pl.BlockSpec((blk, cw), imap)

    rows = nseq * seq_l
    return pl.pallas_call(
        functools.partial(_window_attn_kernel, slopes, dil, seq_l, blk, half),
        out_shape=(jax.ShapeDtypeStruct((rows, dil * cw), F32),
                   jax.ShapeDtypeStruct((rows, dil * HEAD_DIM), F32)),
        grid=(nseq, dil, nlb),
        in_specs=[spec(0, 0), spec(1, -1), spec(1, 0), spec(1, 1), spec(2, -1), spec(2, 0), spec(2, 1)],
        out_specs=(pl.BlockSpec((blk, cw), lambda b, r, lb: (b * nlb + lb, r)),
                   pl.BlockSpec((blk, HEAD_DIM), lambda b, r, lb: (b * nlb + lb, r))),
        compiler_params=_cparams(("arbitrary", "arbitrary", "arbitrary")),
        name="window_attention",
    )(src, src, src, src, src, src, src)


def _undilate(ref, dil, scr):
    if dil == 1:
        return ref[...]
    rows, w = ref.shape[0], ref.shape[1] // dil
    for c in range(w // LANES):
        for r in range(dil):
            c0 = r * w + c * LANES
            scr[c, pl.ds(r, rows, stride=dil), :] = ref[:, c0:c0 + LANES]
    return jnp.concatenate([scr[c] for c in range(w // LANES)], axis=1)


def _window_merge_kernel(o0_ref, o1_ref, o2_ref, l0_ref, l1_ref, l2_ref, gain_ref, out_ref,
                         so1, so2, sl1, sl2):
    dils = tuple(d for _, d in DIL_GROUPS)
    outs = tuple(_undilate(r, d, s) for r, d, s in zip((o0_ref, o1_ref, o2_ref), dils, (None, so1, so2)))
    lses = tuple(_undilate(r, d, s) for r, d, s in zip((l0_ref, l1_ref, l2_ref), dils, (None, sl1, sl2)))
    mx = jnp.maximum(jnp.maximum(lses[0], lses[1]), lses[2])
    es = [jnp.exp(l - mx) for l in lses]
    tot = es[0] + es[1] + es[2]
    for g in range(len(DIL_GROUPS)):
        alpha = es[g] / tot
        for h in range(A_HEADS_PER_GROUP):
            hs = slice(h * HEAD_DIM, (h + 1) * HEAD_DIM)
            c0 = (g * A_HEADS_PER_GROUP + h) * HEAD_DIM
            o = outs[g][:, hs] * alpha[:, h:h + 1]
            out_ref[:, c0:c0 + HEAD_DIM] = _head_norm(o, gain_ref[:, c0:c0 + HEAD_DIM], 1.0).astype(BF16)


def _window_merge(outs, lses, gain_a):
    n = outs[0].shape[0]
    tr = _tile(n, 256)
    dils = tuple(d for _, d in DIL_GROUPS)
    return pl.pallas_call(
        _window_merge_kernel,
        out_shape=jax.ShapeDtypeStruct((n, A_W), BF16),
        grid=(n // tr,),
        in_specs=[pl.BlockSpec((tr // d, d * A_GROUP_W), lambda i: (i, 0)) for d in dils]
        + [pl.BlockSpec((tr // d, d * HEAD_DIM), lambda i: (i, 0)) for d in dils]
        + [pl.BlockSpec((1, A_W), lambda i: (0, 0))],
        out_specs=pl.BlockSpec((tr, A_W), lambda i: (i, 0)),
        scratch_shapes=[pltpu.VMEM((A_GROUP_W // LANES, tr, LANES), F32)] * 2
        + [pltpu.VMEM((HEAD_DIM // LANES, tr, LANES), F32)] * 2,
        compiler_params=_cparams(("arbitrary",)),
        name="window_merge",
    )(*outs, *lses, gain_a)


LOG2E = math.log2(math.e)
ATTN_KV_CHUNK = 1024
DIFF_Q_BLOCK = 512
GQA_Q_BLOCK = 512


def _flash_loop(q, k_ref, v_ref, sa_ref, sb_ref, m_ref, acc_ref, tkc, parts, bias_fn):
    n_chunks = k_ref.shape[0] // tkc
    ones = jnp.ones((tkc, HEAD_DIM), BF16)
    m_ref[...] = jnp.full(m_ref.shape, -jnp.inf, F32)
    acc_ref[...] = jnp.zeros(acc_ref.shape, F32)

    def scores(c):
        ks = pl.ds(pl.multiple_of(c * tkc, tkc), tkc)
        return lax.dot_general(q, k_ref[ks, :], (((1,), (1,)), ((), ())), preferred_element_type=F32)

    def consume(s_ref, c):
        ks = pl.ds(pl.multiple_of(c * tkc, tkc), tkc)
        v_ext = jnp.concatenate([v_ref[ks, :], ones], axis=1)
        bias = bias_fn(c)
        for r0, r1 in parts:
            u = s_ref[r0:r1, :] if bias is None else s_ref[r0:r1, :] + bias
            m_prev = m_ref[r0:r1, :]
            m_new = jnp.maximum(m_prev, jnp.max(u, axis=-1, keepdims=True))
            p = jnp.exp2((u - m_new).astype(BF16))
            alpha = jnp.exp2(m_prev - m_new)
            acc_ref[r0:r1, :] = alpha * acc_ref[r0:r1, :] + jnp.dot(p, v_ext, preferred_element_type=F32)
            m_ref[r0:r1, :] = m_new

    sa_ref[...] = scores(0)

    def body(i, carry):
        c = 2 * i
        sb_ref[...] = scores(c + 1)
        consume(sa_ref, c)
        sa_ref[...] = scores(jnp.minimum(c + 2, n_chunks - 1))
        consume(sb_ref, c + 1)
        return carry

    lax.fori_loop(0, n_chunks // 2, body, 0)


def _softmax_out(acc_ref, rows):
    acc = acc_ref[rows, :]
    return acc[:, :HEAD_DIM] / acc[:, HEAD_DIM:]


def _diff_attn_kernel(tq, tkc, head_scale, slope_ref, lam_ref, gain_ref, q_ref, k_ref, v_ref, o_ref,
                      sa_ref, sb_ref, m_ref, acc_ref):
    q0 = pl.program_id(2) * tq
    q = q_ref[...]
    lane = lax.broadcasted_iota(jnp.int32, q.shape, 1)
    zero = jnp.zeros_like(q)
    q = jnp.concatenate([jnp.where(lane < B_QK_DIM, q, zero), jnp.where(lane >= B_QK_DIM, q, zero)], axis=0)
    slope2 = slope_ref[pl.program_id(1)]
    dij = (lax.broadcasted_iota(jnp.int32, (tq, tkc), 0)
           - lax.broadcasted_iota(jnp.int32, (tq, tkc), 1)).astype(F32)

    def bias_fn(c):
        d = (q0 - c * tkc).astype(F32)
        return -slope2 * jnp.abs(dij + d)

    _flash_loop(q, k_ref, v_ref, sa_ref, sb_ref, m_ref, acc_ref, tkc, ((0, tq), (tq, 2 * tq)), bias_fn)
    o = _softmax_out(acc_ref, slice(0, tq)) - lam_ref[0] * _softmax_out(acc_ref, slice(tq, 2 * tq))
    o_ref[...] = _head_norm(o, gain_ref[...], head_scale).astype(BF16)


def _kv_chunk(seq_t):
    return _tile(seq_t // 2, ATTN_KV_CHUNK)


def _diff_attention(proj, row0, nseq, seq_t, lam, lam_init, gain_b):
    tq = _tile(seq_t, DIFF_Q_BLOCK)
    tkc = _kv_chunk(seq_t)
    nq = seq_t // tq
    slopes2 = jnp.asarray([2.0 ** (-8.0 * (h + 1) / B_HEADS) * LOG2E for h in range(B_HEADS)], F32)
    smem = pl.BlockSpec(memory_space=pltpu.SMEM)
    qcol, kcol, vcol = OFF_QB // HEAD_DIM, OFF_KB // HEAD_DIM, OFF_VB // HEAD_DIM
    return pl.pallas_call(
        functools.partial(_diff_attn_kernel, tq, tkc, 1.0 - lam_init),
        out_shape=jax.ShapeDtypeStruct((nseq * seq_t, B_V_W), BF16),
        grid=(nseq, B_HEADS, nq),
        in_specs=[smem, smem,
                  pl.BlockSpec((1, HEAD_DIM), lambda b, h, qi: (0, h)),
                  pl.BlockSpec((tq, HEAD_DIM), lambda b, h, qi: (row0 // tq + b * nq + qi, qcol + h)),
                  pl.BlockSpec((seq_t, HEAD_DIM), lambda b, h, qi: (row0 // seq_t + b, kcol + h)),
                  pl.BlockSpec((seq_t, HEAD_DIM), lambda b, h, qi: (row0 // seq_t + b, vcol + h))],
        out_specs=pl.BlockSpec((tq, HEAD_DIM), lambda b, h, qi: (b * nq + qi, h)),
        scratch_shapes=[pltpu.VMEM((2 * tq, tkc), F32), pltpu.VMEM((2 * tq, tkc), F32),
                        pltpu.VMEM((2 * tq, 1), F32), pltpu.VMEM((2 * tq, 2 * HEAD_DIM), F32)],
        compiler_params=_cparams(("arbitrary", "arbitrary", "arbitrary")),
        name="diff_attention",
    )(slopes2, lam.reshape(1), gain_b, proj, proj, proj)


def _qk_prep_kernel(proj_ref, gain_ref, scale_ref, cos_ref, sin_ref, o_ref):
    j = pl.program_id(1)
    x = proj_ref[...].astype(F32)
    ms = jnp.mean(x * x, axis=-1, keepdims=True)
    y = x * lax.rsqrt(ms + RMS_EPS) * gain_ref[pl.ds(j, 1), :]
    lane = lax.broadcasted_iota(jnp.int32, y.shape, 1)
    partner = jnp.where((lane & 1) == 0, pltpu.roll(y, HEAD_DIM - 1, 1), pltpu.roll(y, 1, 1))
    o_ref[...] = ((y * cos_ref[...] + partner * sin_ref[...]) * scale_ref[pl.ds(j, 1), :]).astype(BF16)


def _qk_prep(proj, gains, scales, cos_t, sin_t):
    n = proj.shape[0]
    tr = _tile(n, 512)
    nh = C_HEADS + C_KV_HEADS
    c0 = OFF_QC // HEAD_DIM
    full = pl.BlockSpec((nh, HEAD_DIM), lambda i, j: (0, 0))
    tab = pl.BlockSpec((tr, HEAD_DIM), lambda i, j: (i, 0))
    return pl.pallas_call(
        _qk_prep_kernel,
        out_shape=jax.ShapeDtypeStruct((n, nh * HEAD_DIM), BF16),
        grid=(n // tr, nh),
        in_specs=[pl.BlockSpec((tr, HEAD_DIM), lambda i, j: (i, c0 + j)), full, full, tab, tab],
        out_specs=pl.BlockSpec((tr, HEAD_DIM), lambda i, j: (i, j)),
        compiler_params=_cparams(("arbitrary", "arbitrary")),
        name="qk_norm_rope",
    )(proj, gains, scales, cos_t, sin_t)


def _gqa_kernel(tq, tkc, gain_ref, q_ref, k_ref, v_ref, o_ref, sa_ref, sb_ref, m_ref, acc_ref):
    q = jnp.concatenate([q_ref[:, g * HEAD_DIM:(g + 1) * HEAD_DIM] for g in range(C_GROUP)], axis=0)
    _flash_loop(q, k_ref, v_ref, sa_ref, sb_ref, m_ref, acc_ref, tkc, ((0, C_GROUP * tq),), lambda c: None)
    for g in range(C_GROUP):
        gs = slice(g * HEAD_DIM, (g + 1) * HEAD_DIM)
        o = _softmax_out(acc_ref, slice(g * tq, (g + 1) * tq))
        o_ref[:, gs] = _head_norm(o, gain_ref[:, gs], 1.0).astype(BF16)


def _gqa_attention(qk_rot, proj, row0, nseq, seq_t, gain_c):
    tq = _tile(seq_t, GQA_Q_BLOCK)
    tkc = _kv_chunk(seq_t)
    nq = seq_t // tq
    gw = C_GROUP * HEAD_DIM
    vcol = OFF_VC // HEAD_DIM
    rows = C_GROUP * tq
    return pl.pallas_call(
        functools.partial(_gqa_kernel, tq, tkc),
        out_shape=jax.ShapeDtypeStruct((nseq * seq_t, C_Q_W), BF16),
        grid=(nseq, C_KV_HEADS, nq),
        in_specs=[pl.BlockSpec((1, gw), lambda b, h, qi: (0, h)),
                  pl.BlockSpec((tq, gw), lambda b, h, qi: (row0 // tq + b * nq + qi, h)),
                  pl.BlockSpec((seq_t, HEAD_DIM), lambda b, h, qi: (row0 // seq_t + b, C_HEADS + h)),
                  pl.BlockSpec((seq_t, HEAD_DIM), lambda b, h, qi: (row0 // seq_t + b, vcol + h))],
        out_specs=pl.BlockSpec((tq, gw), lambda b, h, qi: (b * nq + qi, h)),
        scratch_shapes=[pltpu.VMEM((rows, tkc), F32), pltpu.VMEM((rows, tkc), F32),
                        pltpu.VMEM((rows, 1), F32), pltpu.VMEM((rows, 2 * HEAD_DIM), F32)],
        compiler_params=_cparams(("arbitrary", "arbitrary", "arbitrary")),
        name="gqa_attention",
    )(gain_c, qk_rot, qk_rot, proj)


def _rope_tables(seq_t):
    rows = seq_t // GRID_W
    r = jnp.repeat(jnp.arange(rows, dtype=F32), GRID_W)
    c = jnp.tile(jnp.arange(GRID_W, dtype=F32), rows)
    half = HEAD_DIM // 2
    inv_freq = ROPE_THETA ** (-jnp.arange(0, half, 2, dtype=F32) / half)
    ang = jnp.concatenate([r[:, None] * inv_freq, c[:, None] * inv_freq], -1)
    cos = jnp.repeat(jnp.cos(ang), 2, axis=-1)
    sin = jnp.repeat(jnp.sin(ang), 2, axis=-1) * jnp.tile(jnp.asarray([-1.0, 1.0], F32), half)
    return cos, sin


def _first_argmax(v, lane, n):
    m = jnp.max(v, axis=-1, keepdims=True)
    idx = jnp.min(jnp.where(v == m, lane, float(n)), axis=-1, keepdims=True)
    return m, idx


def _router_kernel(n_exp, x_ref, wh_ref, wl_ref, bias_ref, idx_ref, wt_ref, cnt_ref):
    x = x_ref[...]
    xh = x.astype(BF16)
    xl = (x - xh.astype(F32)).astype(BF16)
    logits = (jnp.dot(xh, wh_ref[...], preferred_element_type=F32)
              + jnp.dot(xh, wl_ref[...], preferred_element_type=F32)
              + jnp.dot(xl, wh_ref[...], preferred_element_type=F32))
    scores = jax.nn.sigmoid(logits)
    biased = scores + bias_ref[...]
    tr = x.shape[0]
    gsz = n_exp // N_GROUPS
    lane_i = lax.broadcasted_iota(jnp.int32, (tr, n_exp), 1)
    lane = lane_i.astype(F32)
    grp = (lane_i // gsz).astype(F32)
    neg = -jnp.inf
    gscore = jnp.zeros((tr, n_exp), F32)
    for g in range(N_GROUPS):
        in_g = grp == float(g)
        vg = jnp.where(in_g, biased, neg)
        m1, i1 = _first_argmax(vg, lane, n_exp)
        m2 = jnp.max(jnp.where(lane == i1, neg, vg), axis=-1, keepdims=True)
        gscore = jnp.where(in_g, m1 + m2, gscore)
    cand = jnp.full((tr, n_exp), neg, F32)
    for _ in range(TOPK_GROUPS):
        _, i = _first_argmax(gscore, lane, n_exp)
        pick = grp == jnp.floor(i * (1.0 / gsz))
        cand = jnp.where(pick, biased, cand)
        gscore = jnp.where(pick, neg, gscore)
    lane_o = lax.broadcasted_iota(jnp.int32, idx_ref.shape, 1)
    idx_out = jnp.zeros(idx_ref.shape, F32)
    wt_out = jnp.zeros(wt_ref.shape, F32)
    picked = jnp.zeros((tr, n_exp), F32)
    for kk in range(TOP_K):
        _, i = _first_argmax(cand, lane, n_exp)
        pick = lane == i
        sk = jnp.sum(jnp.where(pick, scores, 0.0), axis=-1, keepdims=True)
        cand = jnp.where(pick, neg, cand)
        picked = jnp.where(pick, 1.0, picked)
        idx_out = jnp.where(lane_o == kk, i, idx_out)
        wt_out = jnp.where(lane_o == kk, sk, wt_out)
    idx_ref[...] = idx_out.astype(jnp.int32)
    wt_ref[...] = wt_out / jnp.sum(wt_out, axis=-1, keepdims=True) * ROUTED_SCALE
    per_expert = jnp.sum(picked, axis=0, keepdims=True)
    per_expert = jnp.concatenate([per_expert, jnp.zeros((1, cnt_ref.shape[1] - n_exp), F32)], axis=1)
    cnt_ref[...] = jnp.broadcast_to(per_expert, cnt_ref.shape)


def _router(x, router_w, router_bias):
    n, d = x.shape
    n_exp = router_w.shape[1]
    tr = _tile(n, 256)
    wh = router_w.astype(BF16)
    wl = (router_w - wh.astype(F32)).astype(BF16)
    full = pl.BlockSpec((d, n_exp), lambda i: (0, 0))
    out = pl.BlockSpec((tr, HEAD_DIM), lambda i: (i, 0))
    sub = 8
    idx, wt, cnt = pl.pallas_call(
        functools.partial(_router_kernel, n_exp),
        out_shape=(jax.ShapeDtypeStruct((n, HEAD_DIM), jnp.int32), jax.ShapeDtypeStruct((n, HEAD_DIM), F32),
                   jax.ShapeDtypeStruct((n // tr * sub, LANES), F32)),
        grid=(n // tr,),
        in_specs=[pl.BlockSpec((tr, d), lambda i: (i, 0)), full, full,
                  pl.BlockSpec((1, n_exp), lambda i: (0, 0))],
        out_specs=(out, out, pl.BlockSpec((sub, LANES), lambda i: (i, 0))),
        compiler_params=_cparams(("arbitrary",)),
        name="moe_router",
    )(x, wh, wl, router_bias.reshape(1, n_exp).astype(F32))
    counts = jnp.sum(cnt.reshape(n // tr, sub, LANES)[:, 0, :n_exp], axis=0).astype(jnp.int32)
    return idx[:, :TOP_K], wt[:, :TOP_K], counts


ROW_DMA_UNROLL = 8


def _row_dma(n_rows, copy, wait):
    def one(r, u):
        if wait:
            copy(r).wait()
        else:
            copy(r).start(priority=u % 2)

    def group(i, c):
        for u in range(ROW_DMA_UNROLL):
            one(i * ROW_DMA_UNROLL + u, u)
        return c

    def single(r, c):
        one(r, 0)
        return c

    n_groups = n_rows // ROW_DMA_UNROLL
    lax.fori_loop(0, n_groups, group, 0)
    if not (isinstance(n_rows, int) and n_rows % ROW_DMA_UNROLL == 0):
        lax.fori_loop(n_groups * ROW_DMA_UNROLL, n_rows, single, 0)


def _gate_up_kernel(gather, bm, be_ref, nused_ref, *refs):
    b = pl.program_id(0)
    kc = pl.program_id(1)
    nused = nused_ref[0]
    if gather:
        tok_ref, tok_next_ref, x_hbm, wg_ref, wu_ref, h_ref, gu_acc, xbuf, sem = refs
        slot = b % 2

        def copies(idx_ref, s):
            return lambda r: pltpu.make_async_copy(x_hbm.at[pl.ds(idx_ref[0, 0, r], 1), :],
                                                   xbuf.at[s, pl.ds(r, 1), :], sem.at[s])

        @pl.when((kc == 0) & (b == 0) & (nused > 0))
        def _():
            _row_dma(bm, copies(tok_ref, 0), wait=False)

        @pl.when((kc == 0) & (b + 1 < nused))
        def _():
            _row_dma(bm, copies(tok_next_ref, 1 - slot), wait=False)

        @pl.when((kc == 0) & (b < nused))
        def _():
            _row_dma(bm, copies(tok_ref, slot), wait=True)
    else:
        x_ref, wg_ref, wu_ref, h_ref, gu_acc = refs

    @pl.when(b < nused)
    def _():
        w = xbuf[slot] if gather else x_ref[...]
        x = lax.bitcast_convert_type(jnp.where(kc == 0, w << 16, w & HI_HALF_MASK), F32).astype(BF16)
        ff = h_ref.shape[1]
        main = ff // MXU_COLS * MXU_COLS
        tail = ff - main
        pieces = [jnp.dot(x, wg_ref[0, :, :main], preferred_element_type=F32),
                  jnp.dot(x, wu_ref[0, :, :main], preferred_element_type=F32)]
        if tail:
            w_tail = jnp.concatenate([wg_ref[0, :, main:], wu_ref[0, :, main:]], axis=1)
            pieces.append(jnp.dot(x, w_tail, preferred_element_type=F32))
        bounds = (0, main, 2 * main, 2 * ff)

        @pl.when(kc == 0)
        def _():
            for piece, c0, c1 in zip(pieces, bounds[:-1], bounds[1:]):
                gu_acc[:, c0:c1] = piece

        @pl.when(kc == 1)
        def _():
            tot = [piece + gu_acc[:, c0:c1] for piece, c0, c1 in zip(pieces, bounds[:-1], bounds[1:])]
            h_ref[:, :main] = (tot[0] * jax.nn.sigmoid(tot[0]) * tot[1]).astype(BF16)
            if tail:
                g_t, u_t = tot[2][:, :tail], tot[2][:, tail:]
                h_ref[:, main:] = (g_t * jax.nn.sigmoid(g_t) * u_t).astype(BF16)

    @pl.when((b >= nused) & (kc == 1))
    def _():
        h_ref[...] = jnp.zeros(h_ref.shape, h_ref.dtype)


def _down_rows(h_ref, wlo_ref, whi_ref, rw_ref):
    h = h_ref[...]
    rw = rw_ref[...]
    return _pack_halves(jnp.dot(h, wlo_ref[0], preferred_element_type=F32) * rw,
                        jnp.dot(h, whi_ref[0], preferred_element_type=F32) * rw)


def _down_kernel(be_ref, nused_ref, h_ref, wlo_ref, whi_ref, rw_ref, o_ref):
    o_ref[...] = _down_rows(h_ref, wlo_ref, whi_ref, rw_ref)


def _down_scatter_kernel(be_ref, nused_ref, nreal_ref, h_ref, wlo_ref, whi_ref, rw_ref, dst_ref, dst_prev_ref,
                         y_hbm, obuf, sem):
    b = pl.program_id(0)
    nused = nused_ref[0]
    slot = b % 2
    n_cur = nreal_ref[b]
    n_prev = nreal_ref[jnp.maximum(b - 1, 0)]

    def copies(idx_ref, s):
        return lambda r: pltpu.make_async_copy(obuf.at[s, pl.ds(r, 1), :],
                                               y_hbm.at[pl.ds(idx_ref[0, 0, r], 1), :], sem.at[s])

    @pl.when(b < nused)
    def _():
        obuf[slot] = _down_rows(h_ref, wlo_ref, whi_ref, rw_ref)

    @pl.when((b >= 1) & (b - 1 < nused))
    def _():
        _row_dma(n_prev, copies(dst_prev_ref, 1 - slot), wait=True)

    @pl.when(b < nused)
    def _():
        _row_dma(n_cur, copies(dst_ref, slot), wait=False)

    @pl.when((b == pl.num_programs(0) - 1) & (b < nused))
    def _():
        _row_dma(n_cur, copies(dst_ref, slot), wait=True)


def _expert_ffn(xp, w_gate, w_up, w_down, row_w, block_e, n_used, bm, row_tok=None, row_dst=None, n_out=None):
    dh = xp.shape[1]
    ff = w_gate.shape[-1]
    rows = row_w.shape[0]
    n_blocks = rows // bm
    dispatch = row_tok is not None

    def live(b, nused):
        return jnp.minimum(b, nused[0] - 1)

    def kclamp(b, kc, nused):
        return jnp.where(b < nused[0], kc, 1)

    def idx_spec(shift):
        return pl.BlockSpec((1, 1, bm), lambda b, *_: (jnp.clip(b + shift, 0, n_blocks - 1), 0, 0),
                            memory_space=pltpu.SMEM)

    if dispatch:
        tok = row_tok.reshape(n_blocks, 1, bm)
        x_specs = [idx_spec(0), idx_spec(1), pl.BlockSpec(memory_space=pl.ANY)]
        x_args = (tok, tok, xp)
        x_scratch = [pltpu.VMEM((2, bm, dh), jnp.int32), pltpu.SemaphoreType.DMA((2,))]
    else:
        x_specs = [pl.BlockSpec((bm, dh), lambda b, kc, be, nused: (live(b, nused), 0))]
        x_args = (xp,)
        x_scratch = []
    hmid = pl.pallas_call(
        functools.partial(_gate_up_kernel, dispatch, bm),
        out_shape=jax.ShapeDtypeStruct((rows, ff), BF16),
        grid_spec=pltpu.PrefetchScalarGridSpec(
            num_scalar_prefetch=2,
            grid=(n_blocks, 2),
            in_specs=x_specs + [
                pl.BlockSpec((1, dh, ff), lambda b, kc, be, nused: (be[b], kclamp(b, kc, nused), 0)),
                pl.BlockSpec((1, dh, ff), lambda b, kc, be, nused: (be[b], kclamp(b, kc, nused), 0))],
            out_specs=pl.BlockSpec((bm, ff), lambda b, kc, be, nused: (b, 0)),
            scratch_shapes=[pltpu.VMEM((bm, 2 * ff), F32)] + x_scratch,
        ),
        compiler_params=_cparams(("arbitrary", "arbitrary")),
        name="expert_gate_up",
    )(block_e, n_used, *x_args, w_gate, w_up)

    rw = row_w.reshape(rows, 1)
    if not dispatch:
        tn = _tile(dh, 1024)
        nn = dh // tn
        return pl.pallas_call(
            _down_kernel,
            out_shape=jax.ShapeDtypeStruct((rows, dh), jnp.int32),
            grid_spec=pltpu.PrefetchScalarGridSpec(
                num_scalar_prefetch=2,
                grid=(n_blocks, nn),
                in_specs=[pl.BlockSpec((bm, ff), lambda b, j, be, nused: (b, 0)),
                          pl.BlockSpec((1, ff, tn), lambda b, j, be, nused: (be[b], 0, j)),
                          pl.BlockSpec((1, ff, tn), lambda b, j, be, nused: (be[b], 0, nn + j)),
                          pl.BlockSpec((bm, 1), lambda b, j, be, nused: (b, 0))],
                out_specs=pl.BlockSpec((bm, tn), lambda b, j, be, nused: (b, j)),
            ),
            compiler_params=_cparams(("arbitrary", "arbitrary")),
            name="expert_down",
        )(block_e, n_used, hmid, w_down, w_down, rw)

    dst = row_dst.reshape(n_blocks, 1, bm)
    return pl.pallas_call(
        _down_scatter_kernel,
        out_shape=jax.ShapeDtypeStruct((n_out, dh), jnp.int32),
        grid_spec=pltpu.PrefetchScalarGridSpec(
            num_scalar_prefetch=3,
            grid=(n_blocks,),
            in_specs=[pl.BlockSpec((bm, ff), lambda b, be, nused, nreal: (live(b, nused), 0)),
                      pl.BlockSpec((1, ff, dh), lambda b, be, nused, nreal: (be[b], 0, 0)),
                      pl.BlockSpec((1, ff, dh), lambda b, be, nused, nreal: (be[b], 0, 1)),
                      pl.BlockSpec((bm, 1), lambda b, be, nused, nreal: (live(b, nused), 0)),
                      idx_spec(0), idx_spec(-1)],
            out_specs=pl.BlockSpec(memory_space=pl.ANY),
            scratch_shapes=[pltpu.VMEM((2, bm, dh), jnp.int32), pltpu.SemaphoreType.DMA((2,))],
        ),
        compiler_params=_cparams(("arbitrary",)),
        name="expert_down_scatter",
    )(block_e, n_used, jnp.sum((dst >= 0).astype(jnp.int32), axis=(1, 2)), hmid, w_down, w_down, rw, dst, dst)


def _combine_ln_kernel(alpha, x_ref, g_ref, b_ref, *refs):
    packed_refs, (of_ref, ob_ref) = refs[:-2], refs[-2:]
    lo = hi = None
    for ref in packed_refs:
        w = ref[...]
        lo = _unpack_lo(w) if lo is None else lo + _unpack_lo(w)
        hi = _unpack_hi(w) if hi is None else hi + _unpack_hi(w)
    x = alpha * x_ref[...] + jnp.concatenate([lo, hi], axis=1)
    y = _layer_norm_rows(x, g_ref, b_ref)
    of_ref[...] = y
    ob_ref[...] = y.astype(BF16)


def _combine_ln(x, shared, y_slots, alpha, g, b):
    n, d = x.shape
    tr = _tile(n, 128)
    nt = n // tr
    row = pl.BlockSpec((tr, d), lambda i: (i, 0))
    vec = pl.BlockSpec((1, d), lambda i: (0, 0))
    slots = [pl.BlockSpec((tr, d // 2), functools.partial(lambda k, i: (k * nt + i, 0), k)) for k in range(TOP_K)]
    return pl.pallas_call(
        functools.partial(_combine_ln_kernel, alpha),
        out_shape=(jax.ShapeDtypeStruct((n, d), F32), jax.ShapeDtypeStruct((n, d), BF16)),
        grid=(nt,),
        in_specs=[row, vec, vec, pl.BlockSpec((tr, d // 2), lambda i: (i, 0))] + slots,
        out_specs=(row, row),
        compiler_params=_cparams(("arbitrary",)),
        name="moe_combine_layer_norm",
    )(x, g.reshape(1, d), b.reshape(1, d), shared, *([y_slots] * TOP_K))


def _dispatch_plan(idx, wts, counts, bm):
    n = idx.shape[0]
    n_exp = counts.shape[0]
    nk = n * TOP_K
    n_blocks = -(-nk // bm) + n_exp
    rows = n_blocks * bm
    flat_e = idx.reshape(-1)
    order = jnp.argsort(flat_e, stable=True).astype(jnp.int32)
    padded = (counts + bm - 1) // bm * bm
    starts = jnp.cumsum(counts) - counts
    pends = jnp.cumsum(padded)
    pstarts = pends - padded
    blk_start = jnp.arange(n_blocks, dtype=jnp.int32) * bm
    block_e = jnp.minimum(jnp.sum((pends[None, :] <= blk_start[:, None]).astype(jnp.int32), axis=1), n_exp - 1)
    n_used = (pends[-1] // bm).astype(jnp.int32).reshape(1)
    row = jnp.arange(rows, dtype=jnp.int32)
    row_e = jnp.repeat(lax.optimization_barrier(block_e), bm)
    off = row - pstarts[row_e]
    valid = off < counts[row_e]
    pair = order[jnp.clip(starts[row_e] + off, 0, nk - 1)]
    row_tok = jnp.where(valid, pair // TOP_K, 0)
    row_w = jnp.where(valid, wts.reshape(-1)[pair], 0.0)
    row_dst = jnp.where(valid, (pair % TOP_K) * n + pair // TOP_K, -1)
    return row_tok, row_w, row_dst, block_e, n_used


def _token_mixer(xb, seqs, layer, lam_init, w_in, w_out, out_gain_l, qk_gains, qk_scales, lam, cos_t, sin_t):
    proj = _matmul([xb], w_in, layer, BF16)
    qk_rot = _qk_prep(proj, qk_gains, qk_scales, cos_t, sin_t)
    window_src = [proj if dil == 1 else _dilate(proj, g) for g, (_, dil) in enumerate(DIL_GROUPS)]
    gain = out_gain_l.reshape(1, MIX_WIDTH).astype(F32)
    gain_a, gain_b, gain_c = gain[:, :A_W], gain[:, A_W:A_W + B_V_W], gain[:, A_W + B_V_W:]
    a_parts, b_parts, c_parts = [], [], []
    for row0, nseq, seq_t in seqs:
        outs, lses = zip(*[_window_attention(window_src[g], row0, nseq, seq_t, g)
                           for g in range(len(DIL_GROUPS))])
        a_parts.append(_window_merge(outs, lses, gain_a))
        b_parts.append(_diff_attention(proj, row0, nseq, seq_t, lam, lam_init, gain_b))
        c_parts.append(_gqa_attention(qk_rot, proj, row0, nseq, seq_t, gain_c))
    parts = [jnp.concatenate(p, axis=0) for p in (a_parts, b_parts, c_parts)]
    return _matmul(parts, w_out, layer, F32)


def _moe(xf, xp, layer, router_w, router_bias, w_gate, w_up, w_down, sh_gate, sh_up, sh_down, bm):
    n = xf.shape[0]
    n_exp = router_w.shape[1]
    idx, wts, counts = _router(xf, router_w, router_bias)
    row_tok, row_w, row_dst, block_e, n_used = _dispatch_plan(idx, wts, counts, bm)
    y_slots = _expert_ffn(xp, w_gate, w_up, w_down, row_w, block_e + layer * n_exp, n_used, bm,
                          row_tok=row_tok, row_dst=row_dst, n_out=TOP_K * n)
    sb = _tile(n, bm)
    shared = _expert_ffn(xp, sh_gate, sh_up, sh_down, jnp.ones((n,), F32),
                         jnp.full((n // sb,), layer, jnp.int32), jnp.full((1,), n // sb, jnp.int32), sb)
    return y_slots, shared


def _forward(x_prompt, x_sample, ln_in_g, ln_in_b, w_in, w_out, out_gain, q_norm, k_norm,
             lambda_q1, lambda_k1, lambda_q2, lambda_k2, ln1_g, ln1_b, router_w, router_bias,
             w_gate, w_up, w_down, shared_gate, shared_up, shared_down, ln2_g, ln2_b, moe_block=512):
    depth = w_in.shape[0]
    d = x_prompt.shape[-1]
    alpha = (2 * depth) ** 0.25
    bp, tp, _ = x_prompt.shape
    bs, ts, _ = x_sample.shape
    n_s = bs * ts
    assert ts >= tp and n_s % tp == 0
    seqs = ((0, bs, ts), (n_s, bp, tp))
    x = jnp.concatenate([x_sample.reshape(n_s, d), x_prompt.reshape(bp * tp, d)], axis=0)

    tabs = [_rope_tables(t) for t in (ts, tp)]
    cos_t = jnp.concatenate([jnp.tile(tabs[0][0], (bs, 1)), jnp.tile(tabs[1][0], (bp, 1))], axis=0)
    sin_t = jnp.concatenate([jnp.tile(tabs[0][1], (bs, 1)), jnp.tile(tabs[1][1], (bp, 1))], axis=0)
    qk_scales = jnp.concatenate([jnp.full((C_HEADS, HEAD_DIM), HEAD_DIM ** -0.5 * LOG2E, F32),
                                 jnp.ones((C_KV_HEADS, HEAD_DIM), F32)], axis=0)
    in_col_scale = jnp.ones((IN_WIDTH,), F32).at[OFF_QB:OFF_KB].set(B_QK_DIM ** -0.5 * LOG2E)

    w_in_b = (w_in * in_col_scale).astype(BF16)
    w_out_b = w_out.astype(BF16)
    expert_w = [w.astype(BF16).reshape((depth * w.shape[1],) + w.shape[2:]) for w in (w_gate, w_up, w_down)]
    shared_w = [w.astype(BF16) for w in (shared_gate, shared_up, shared_down)]

    xf, xb = _layer_norm(x, [], 1.0, ln_in_g, ln_in_b)
    for l in range(depth):
        lam_init = 0.8 - 0.6 * math.exp(-0.3 * l)
        lam = (jnp.exp(jnp.sum(lambda_q1[l].astype(F32) * lambda_k1[l].astype(F32)))
               - jnp.exp(jnp.sum(lambda_q2[l].astype(F32) * lambda_k2[l].astype(F32))) + lam_init)
        qk_gains = jnp.concatenate([jnp.tile(q_norm[l][None].astype(F32), (C_HEADS, 1)),
                                    jnp.tile(k_norm[l][None].astype(F32), (C_KV_HEADS, 1))], axis=0)
        mix = _token_mixer(xb, seqs, l, lam_init, w_in_b, w_out_b, out_gain[l],
                           qk_gains, qk_scales, lam, cos_t, sin_t)
        xf, xp = _layer_norm(xf, [mix], alpha, ln1_g[l], ln1_b[l], packed=True)
        y_slots, shared = _moe(xf, xp, l, router_w[l], router_bias[l], *expert_w, *shared_w, moe_block)
        xf, xb = _combine_ln(xf, shared, y_slots, alpha, ln2_g[l], ln2_b[l])
    return xf[n_s:].reshape(bp, tp, d), xf[:n_s].reshape(bs, ts, d)


def kernel(x_prompt, x_sample, ln_in_g, ln_in_b, w_in, w_out, out_gain, q_norm, k_norm, lambda_q1, lambda_k1,
           lambda_q2, lambda_k2, ln1_g, ln1_b, router_w, router_bias, w_gate, w_up, w_down, shared_gate,
           shared_up, shared_down, ln2_g, ln2_b):
    return _forward(x_prompt, x_sample, ln_in_g, ln_in_b, w_in, w_out, out_gain, q_norm, k_norm,
                    lambda_q1, lambda_k1, lambda_q2, lambda_k2, ln1_g, ln1_b, router_w, router_bias,
                    w_gate, w_up, w_down, shared_gate, shared_up, shared_down, ln2_g, ln2_b)
```
